```python
import math
import jax, jax.numpy as jnp
from jax import lax
import numpy as np

D_MODEL = 1024
BATCH = 32
SEQ = 2048
DEPTH = 1

CHUNK = 64
N_META = 16
N_HEADS = 8
HEAD_DIM = 64
ROT_DIM = HEAD_DIM // 4
ROPE_THETA = 500000.0
Q_BLOCK = 128
ATTN_W = N_HEADS * 2 * HEAD_DIM
CONV_CH = D_MODEL
CONV_K = 31
N_GROUPS = 4
EXPERTS_PER_GROUP = 8
N_EXPERTS = N_GROUPS * EXPERTS_PER_GROUP
TOP_K = 2
D_EXPERT = D_MODEL // 2
MOE_BLOCK = 512
IN_COLS = 3 * ATTN_W + 2 * CONV_CH + 2 * D_MODEL
EPS = 1e-6

kernel_name = "hybrid_diffattn_conformer_hiermoe"


def rmsnorm(x, g):
    xf = x.astype(jnp.float32)
    y = xf * lax.rsqrt(jnp.mean(xf * xf, axis=-1, keepdims=True) + EPS)
    return (y * g.astype(jnp.float32)).astype(x.dtype)


def layernorm(x, g, b):
    xf = x.astype(jnp.float32)
    mu = jnp.mean(xf, axis=-1, keepdims=True)
    var = jnp.mean(jnp.square(xf - mu), axis=-1, keepdims=True)
    y = (xf - mu) * lax.rsqrt(var + EPS)
    return (y * g.astype(jnp.float32) + b.astype(jnp.float32)).astype(x.dtype)


def chunk_ids(n):
    p = jnp.arange(n)
    return jnp.where(p < N_META, 0, 1 + (p - N_META) // CHUNK)


def partial_rope(x):
    L = x.shape[1]
    half = ROT_DIM // 2
    inv_freq = ROPE_THETA ** (-jnp.arange(0, ROT_DIM, 2, dtype=jnp.float32) / ROT_DIM)
    ang = jnp.arange(L, dtype=jnp.float32)[:, None] * inv_freq[None, :]
    cos = jnp.cos(ang)[:, None, None, :]
    sin = jnp.sin(ang)[:, None, None, :]
    xr = x[..., :ROT_DIM].astype(jnp.float32)
    x1, x2 = xr[..., :half], xr[..., half:]
    rot = jnp.concatenate([x1 * cos - x2 * sin, x2 * cos + x1 * sin], axis=-1)
    return jnp.concatenate([rot.astype(x.dtype), x[..., ROT_DIM:]], axis=-1)


def diff_attention(q, k, v, lam):
    B, L = q.shape[0], q.shape[1]
    Lp = -(-L // Q_BLOCK) * Q_BLOCK
    pad = Lp - L
    q = jnp.pad(q, ((0, 0), (0, pad), (0, 0), (0, 0), (0, 0)))
    k = jnp.pad(k, ((0, 0), (0, pad), (0, 0), (0, 0), (0, 0)))
    v = jnp.pad(v, ((0, 0), (0, pad), (0, 0), (0, 0)))
    cid = chunk_ids(Lp)
    nb = Lp // Q_BLOCK
    qb = q.reshape(B, nb, Q_BLOCK, N_HEADS, 2, HEAD_DIM).transpose(1, 0, 2, 3, 4, 5)
    cq = cid.reshape(nb, Q_BLOCK)
    scale = HEAD_DIM ** -0.5

    def one_block(args):
        qi, ci = args
        s = jnp.einsum('bqhmd,bkhmd->bhmqk', qi, k,
                       preferred_element_type=jnp.float32) * scale
        mask = cid[None, :] <= ci[:, None]
        s = jnp.where(mask, s, -jnp.inf)
        p = jax.nn.softmax(s, axis=-1)
        a = p[:, :, 0] - lam * p[:, :, 1]
        return jnp.einsum('bhqk,bkhe->bqhe', a.astype(v.dtype), v)

    o = lax.map(one_block, (qb, cq))
    o = o.transpose(1, 0, 2, 3, 4).reshape(B, Lp, N_HEADS, 2 * HEAD_DIM)
    return o[:, :L]


def conformer_conv(u, conv_w, conv_b, ln_g, ln_b, w_pw2):
    a, b = jnp.split(u, 2, axis=-1)
    z = a * jax.nn.sigmoid(b)
    z = lax.conv_general_dilated(z, conv_w[:, None, :].astype(z.dtype), window_strides=(1,),
                                 padding=[(CONV_K - 1, 0)],
                                 dimension_numbers=('NWC', 'WIO', 'NWC'),
                                 feature_group_count=CONV_CH)
    z = z + conv_b
    z = layernorm(z, ln_g, ln_b)
    z = jax.nn.silu(z)
    return z @ w_pw2


def hier_moe(h, w_group, b_group, w_router, b_router, w_gate, w_up, w_down):
    Bn, L, D = h.shape
    N = Bn * L
    hf = h.reshape(N, D)
    hf32 = hf.astype(jnp.float32)
    g_prob = jax.nn.softmax(hf32 @ w_group.astype(jnp.float32) + b_group.astype(jnp.float32), axis=-1)
    g_w, g_idx = lax.top_k(g_prob, 1)
    e_logits = (hf32 @ w_router.astype(jnp.float32) + b_router.astype(jnp.float32))
    e_logits = e_logits.reshape(N, N_GROUPS, EXPERTS_PER_GROUP)
    e_in = jnp.take_along_axis(e_logits, g_idx[:, :, None], axis=1)[:, 0]
    e_w, e_loc = lax.top_k(jax.nn.softmax(e_in, axis=-1), TOP_K)
    e_w = e_w / jnp.sum(e_w, axis=-1, keepdims=True)
    gate = g_w * e_w
    eid = g_idx * EXPERTS_PER_GROUP + e_loc
    A = N * TOP_K
    flat_e = eid.reshape(A)
    flat_t = jnp.repeat(jnp.arange(N, dtype=jnp.int32), TOP_K)
    flat_w = gate.reshape(A)
    order = jnp.argsort(flat_e)
    se, st, sw = flat_e[order], flat_t[order], flat_w[order]
    counts = jnp.bincount(flat_e, length=N_EXPERTS)
    starts = jnp.cumsum(counts) - counts
    pcounts = (counts + MOE_BLOCK - 1) // MOE_BLOCK * MOE_BLOCK
    pends = jnp.cumsum(pcounts)
    pstarts = pends - pcounts
    dest = pstarts[se] + jnp.arange(A) - starts[se]
    nb = -(-A // MOE_BLOCK) + N_EXPERTS
    P = nb * MOE_BLOCK
    buf_t = jnp.full((P,), N, dtype=jnp.int32).at[dest].set(st)
    buf_w = jnp.zeros((P,), dtype=h.dtype).at[dest].set(sw.astype(h.dtype))
    blk_e = jnp.minimum(jnp.searchsorted(pends, jnp.arange(nb) * MOE_BLOCK, side='right'),
                        N_EXPERTS - 1)
    h_pad = jnp.concatenate([hf, jnp.zeros((1, D), hf.dtype)], axis=0)

    def expert_block(args):
        tok, e = args
        xb = h_pad[tok]
        return (jax.nn.silu(xb @ w_gate[e]) * (xb @ w_up[e])) @ w_down[e]

    y = lax.map(expert_block, (buf_t.reshape(nb, MOE_BLOCK), blk_e)).reshape(P, D)
    out = jnp.zeros((N + 1, D), h.dtype).at[buf_t].add(y * buf_w[:, None])[:N]
    return out.reshape(Bn, L, D)


def setup_inputs(seed: int = 0) -> dict:
    key = jax.random.key(seed)
    ks = jax.random.split(key, 32)
    f32 = jnp.float32
    nrm = lambda k, shape, s: jax.random.normal(k, shape, f32) * s
    Dd = DEPTH
    return {
        "x": nrm(ks[0], (BATCH, SEQ, D_MODEL), 1.0),
        "meta_tokens": nrm(ks[1], (N_META, D_MODEL), 1.0),
        "norm1_g": 1.0 + nrm(ks[2], (Dd, D_MODEL), 0.01),
        "w_in": nrm(ks[3], (Dd, D_MODEL, IN_COLS), D_MODEL ** -0.5),
        "lam_q1": nrm(ks[4], (Dd, HEAD_DIM), 0.1),
        "lam_k1": nrm(ks[5], (Dd, HEAD_DIM), 0.1),
        "lam_q2": nrm(ks[6], (Dd, HEAD_DIM), 0.1),
        "lam_k2": nrm(ks[7], (Dd, HEAD_DIM), 0.1),
        "subln_g": 1.0 + nrm(ks[8], (Dd, 2 * HEAD_DIM), 0.01),
        "w_o_attn": nrm(ks[9], (Dd, ATTN_W, D_MODEL), ATTN_W ** -0.5),
        "conv_w": nrm(ks[10], (Dd, CONV_K, CONV_CH), CONV_K ** -0.5),
        "conv_b": nrm(ks[11], (Dd, CONV_CH), 0.01),
        "conv_ln_g": 1.0 + nrm(ks[12], (Dd, CONV_CH), 0.01),
        "conv_ln_b": nrm(ks[13], (Dd, CONV_CH), 0.01),
        "w_pw2": nrm(ks[14], (Dd, CONV_CH, D_MODEL), CONV_CH ** -0.5),
        "w_out": nrm(ks[15], (Dd, D_MODEL, D_MODEL), D_MODEL ** -0.5),
        "norm2_g": 1.0 + nrm(ks[16], (Dd, D_MODEL), 0.01),
        "w_group": nrm(ks[17], (Dd, D_MODEL, N_GROUPS), D_MODEL ** -0.5),
        "b_group": nrm(ks[18], (Dd, N_GROUPS), 0.01),
        "w_router": nrm(ks[19], (Dd, D_MODEL, N_EXPERTS), D_MODEL ** -0.5),
        "b_router": nrm(ks[20], (Dd, N_EXPERTS), 0.01),
        "w_e_gate": nrm(ks[21], (Dd, N_EXPERTS, D_MODEL, D_EXPERT), D_MODEL ** -0.5),
        "w_e_up": nrm(ks[22], (Dd, N_EXPERTS, D_MODEL, D_EXPERT), D_MODEL ** -0.5),
        "w_e_down": nrm(ks[23], (Dd, N_EXPERTS, D_EXPERT, D_MODEL), D_EXPERT ** -0.5),
        "final_g": 1.0 + nrm(ks[24], (D_MODEL,), 0.01),
    }


def reference(x, meta_tokens, norm1_g, w_in, lam_q1, lam_k1, lam_q2, lam_k2, subln_g,
              w_o_attn, conv_w, conv_b, conv_ln_g, conv_ln_b, w_pw2, w_out, norm2_g,
              w_group, b_group, w_router, b_router, w_e_gate, w_e_up, w_e_down, final_g):
    B = x.shape[0]
    meta = jnp.broadcast_to(meta_tokens.astype(x.dtype)[None], (B, N_META, D_MODEL))
    h_res = jnp.concatenate([meta, x], axis=1)
    L = h_res.shape[1]
    for l in range(DEPTH):
        lam_init = 0.8 - 0.6 * math.exp(-0.3 * l)
        h = rmsnorm(h_res, norm1_g[l])
        proj = h @ w_in[l]
        q, k, v, u, gl = jnp.split(
            proj, [ATTN_W, 2 * ATTN_W, 3 * ATTN_W, 3 * ATTN_W + 2 * CONV_CH], axis=-1)
        q = partial_rope(q.reshape(B, L, N_HEADS, 2, HEAD_DIM))
        k = partial_rope(k.reshape(B, L, N_HEADS, 2, HEAD_DIM))
        v = v.reshape(B, L, N_HEADS, 2 * HEAD_DIM)
        lam = (jnp.exp(jnp.sum(lam_q1[l].astype(jnp.float32) * lam_k1[l].astype(jnp.float32)))
               - jnp.exp(jnp.sum(lam_q2[l].astype(jnp.float32) * lam_k2[l].astype(jnp.float32)))
               + lam_init)
        o = diff_attention(q, k, v, lam)
        o = rmsnorm(o, subln_g[l]) * (1.0 - lam_init)
        y_attn = o.reshape(B, L, ATTN_W) @ w_o_attn[l]
        y_conv = conformer_conv(u, conv_w[l], conv_b[l], conv_ln_g[l], conv_ln_b[l], w_pw2[l])
        g_attn, g_conv = jnp.split(jax.nn.sigmoid(gl), 2, axis=-1)
        h_res = h_res + (g_attn * y_attn + g_conv * y_conv) @ w_out[l]
        h2 = rmsnorm(h_res, norm2_g[l])
        h_res = h_res + hier_moe(h2, w_group[l], b_group[l], w_router[l], b_router[l],
                                 w_e_gate[l], w_e_up[l], w_e_down[l])
    y = rmsnorm(h_res, final_g)
    return y[:, N_META:]
```

```python
import functools
import math

import jax
import jax.numpy as jnp
from jax import lax
from jax.experimental import pallas as pl
from jax.experimental.pallas import tpu as pltpu

F32 = jnp.float32
BF16 = jnp.bfloat16

D_MODEL = 1024
CHUNK = 64
N_META = 16
N_HEADS = 8
HEAD_DIM = 64
ROT_DIM = HEAD_DIM // 4
ROPE_THETA = 500000.0
ATTN_W = N_HEADS * 2 * HEAD_DIM
CONV_K = 31
N_GROUPS = 4
EXPERTS_PER_GROUP = 8
N_EXPERTS = N_GROUPS * EXPERTS_PER_GROUP
TOP_K = 2
D_EXPERT = D_MODEL // 2
MOE_BLOCK = 512
EPS = 1e-6
LANES = 128
SUBLANES = 8
ROW_TILES = D_MODEL // LANES
N_PROJ = 7
CONV_HALO = 32
VMEM_LIMIT = 48 * 1024 * 1024


def _cparams(sem):
    return pltpu.CompilerParams(dimension_semantics=sem, vmem_limit_bytes=VMEM_LIMIT)


def _rope(t, c, s_lo, s_hi):
    half = ROT_DIM // 2
    n = t.shape[-1]
    return t * c + pltpu.roll(t, n - half, 1) * s_lo + pltpu.roll(t, half, 1) * s_hi


def _inproj_body(x_ref, g_ref, w_ref, rc_ref, rlo_ref, rhi_ref,
                 q_ref, k_ref, v_ref, z_ref, ga_ref, gc_ref, xn_ref, a_ref):
    j = pl.program_id(1)

    @pl.when(j == 0)
    def _():
        x = x_ref[...]
        ms = jnp.mean(x * x, axis=-1, keepdims=True)
        xn_ref[...] = ((x * lax.rsqrt(ms + EPS)) * g_ref[...]).astype(BF16)

    acc = jnp.dot(xn_ref[...], w_ref[...], preferred_element_type=F32)

    def tables():
        reps = (1, D_MODEL // LANES)
        return (jnp.tile(rc_ref[...], reps), jnp.tile(rlo_ref[...], reps), jnp.tile(rhi_ref[...], reps))

    @pl.when(j == 0)
    def _():
        c, lo, hi = tables()
        q_ref[...] = (_rope(acc, c, lo, hi) * (HEAD_DIM ** -0.5)).astype(BF16)

    @pl.when(j == 1)
    def _():
        c, lo, hi = tables()
        k_ref[...] = _rope(acc, c, lo, hi).astype(BF16)

    @pl.when(j == 2)
    def _():
        v_ref[...] = acc.astype(BF16)

    @pl.when(j == 3)
    def _():
        a_ref[...] = acc

    @pl.when(j == 4)
    def _():
        z_ref[...] = (a_ref[...] * jax.nn.sigmoid(acc)).astype(BF16)

    @pl.when(j == 5)
    def _():
        ga_ref[...] = jax.nn.sigmoid(acc).astype(BF16)

    @pl.when(j == 6)
    def _():
        gc_ref[...] = jax.nn.sigmoid(acc).astype(BF16)


def _inproj(x2d, g, w_bf16, rc, rlo, rhi, tm):
    rows = x2d.shape[0]
    tab_blocks = rc.shape[0] // tm
    row_spec = pl.BlockSpec((tm, D_MODEL), lambda i, j: (i, 0))
    tab_spec = pl.BlockSpec((tm, LANES), lambda i, j: (i % tab_blocks, 0))
    out = jax.ShapeDtypeStruct((rows, D_MODEL), BF16)
    return pl.pallas_call(
        _inproj_body,
        grid=(rows // tm, N_PROJ),
        in_specs=[row_spec,
                  pl.BlockSpec((1, D_MODEL), lambda i, j: (0, 0)),
                  pl.BlockSpec((D_MODEL, D_MODEL), lambda i, j: (0, j)),
                  tab_spec, tab_spec, tab_spec],
        out_specs=[row_spec] * 6,
        out_shape=[out] * 6,
        scratch_shapes=[pltpu.VMEM((tm, D_MODEL), BF16), pltpu.VMEM((tm, D_MODEL), F32)],
        compiler_params=_cparams(("arbitrary", "arbitrary")),
        name="inproj",
    )(x2d, g, w_bf16, rc, rlo, rhi)


def _rope_tables(length):
    half = ROT_DIM // 2
    inv_freq = ROPE_THETA ** (-jnp.arange(0, ROT_DIM, 2, dtype=F32) / ROT_DIM)
    ang = jnp.arange(length, dtype=F32)[:, None] * inv_freq[None, :]
    cos, sin = jnp.cos(ang), jnp.sin(ang)
    ones = jnp.ones((length, HEAD_DIM - ROT_DIM), F32)
    zeros = jnp.zeros((length, HEAD_DIM - half), F32)
    c = jnp.concatenate([cos, cos, ones], axis=1)
    lo = jnp.concatenate([-sin, zeros], axis=1)
    hi = jnp.concatenate([jnp.zeros((length, half), F32), sin, jnp.zeros((length, HEAD_DIM - ROT_DIM), F32)], axis=1)
    rep = LANES // HEAD_DIM
    return jnp.tile(c, (1, rep)), jnp.tile(lo, (1, rep)), jnp.tile(hi, (1, rep))


def _dot_nt(a, b):
    return lax.dot_general(a, b, (((1,), (1,)), ((), ())), preferred_element_type=F32)


def _attn_body(lq1_ref, lk1_ref, lq2_ref, lk2_ref, sg_ref, q_ref, k_ref, v_ref, km_ref, vm_ref,
               o_ref, m_ref, l_ref, acc_ref, *, seq, tq, lam_init):
    lam = (jnp.exp(jnp.sum(lq1_ref[...] * lk1_ref[...], keepdims=True))
           - jnp.exp(jnp.sum(lq2_ref[...] * lk2_ref[...], keepdims=True)) + lam_init)
    lane = lax.broadcasted_iota(jnp.int32, (tq, LANES), 1)
    row_chunk = (lax.broadcasted_iota(jnp.int32, (2 * tq, tq), 0) % tq) // CHUNK
    col_chunk = lax.broadcasted_iota(jnp.int32, (2 * tq, tq), 1) // CHUNK
    diag_mask = col_chunk <= row_chunk
    km = km_ref[...]
    vm = vm_ref[...]

    def online_step(s, vb):
        m_old = m_ref[...]
        m_new = jnp.maximum(m_old, jnp.max(s, axis=-1, keepdims=True))
        alpha = jnp.exp(m_old - m_new)
        p = jnp.exp(s - m_new)
        l_ref[...] = alpha * l_ref[...] + jnp.sum(p, axis=-1, keepdims=True)
        acc_ref[...] = alpha * acc_ref[...] + jnp.dot(p.astype(BF16), vb, preferred_element_type=F32)
        m_ref[...] = m_new

    for qi in range(seq // tq):
        q = q_ref[qi * tq:(qi + 1) * tq, :]
        zero = jnp.zeros_like(q)
        q2 = jnp.concatenate([jnp.where(lane < HEAD_DIM, q, zero), jnp.where(lane >= HEAD_DIM, q, zero)], axis=0)

        s = _dot_nt(q2, km)
        m0 = jnp.max(s, axis=-1, keepdims=True)
        p = jnp.exp(s - m0)
        m_ref[...] = m0
        l_ref[...] = jnp.sum(p, axis=-1, keepdims=True)
        acc_ref[...] = jnp.dot(p.astype(BF16), vm, preferred_element_type=F32)

        def full_block(j, carry):
            start = pl.multiple_of(j * tq, tq)
            online_step(_dot_nt(q2, k_ref[pl.ds(start, tq), :]), v_ref[pl.ds(start, tq), :])
            return carry

        lax.fori_loop(0, qi, full_block, 0)

        s = _dot_nt(q2, k_ref[qi * tq:(qi + 1) * tq, :])
        online_step(jnp.where(diag_mask, s, -jnp.inf), v_ref[qi * tq:(qi + 1) * tq, :])

        o_maps = acc_ref[...] / l_ref[...]
        o = o_maps[:tq] - lam * o_maps[tq:]
        ms = jnp.mean(o * o, axis=-1, keepdims=True)
        o = ((o * lax.rsqrt(ms + EPS)) * sg_ref[...]) * (1.0 - lam_init)
        o_ref[qi * tq:(qi + 1) * tq, :] = o.astype(BF16)


def _attention(q, k, v, k_meta, v_meta, lq1, lk1, lq2, lk2, subln_g, batch, seq, lam_init):
    tq = min(256, seq)
    head_spec = pl.BlockSpec((seq, LANES), lambda b, h: (b, h))
    meta_spec = pl.BlockSpec((N_META, LANES), lambda b, h: (0, h))
    lam_spec = pl.BlockSpec((1, HEAD_DIM), lambda b, h: (0, 0))
    body = functools.partial(_attn_body, seq=seq, tq=tq, lam_init=lam_init)
    return pl.pallas_call(
        body,
        grid=(batch, N_HEADS),
        in_specs=[lam_spec] * 4 + [pl.BlockSpec((1, LANES), lambda b, h: (0, 0)),
                                   head_spec, head_spec, head_spec, meta_spec, meta_spec],
        out_specs=head_spec,
        out_shape=jax.ShapeDtypeStruct(q.shape, BF16),
        scratch_shapes=[pltpu.VMEM((2 * tq, 1), F32), pltpu.VMEM((2 * tq, 1), F32),
                        pltpu.VMEM((2 * tq, LANES), F32)],
        compiler_params=_cparams(("arbitrary", "arbitrary")),
        name="attn",
    )(lq1, lk1, lq2, lk2, subln_g, q, k, v, k_meta, v_meta)


def _post_body(o_ref, z_ref, zprev_ref, zmeta_ref, ga_ref, gc_ref, x_ref,
               wo_ref, wpw_ref, wout_ref, cw_ref, cb_ref, lng_ref, lnb_ref, n2g_ref, wr_ref, br_ref,
               hres_ref, h2_ref, route_ref, zext_ref, zc_ref, *, tm, tiles_per_seq, conv_rows):
    i = pl.program_id(0)
    first = (i % tiles_per_seq) == 0
    meta_halo = jnp.concatenate(
        [jnp.zeros((CONV_HALO - N_META, D_MODEL), F32), zmeta_ref[...].astype(F32)], axis=0)
    zext_ref[0:CONV_HALO, :] = jnp.where(first, meta_halo, zprev_ref[...].astype(F32))
    zext_ref[CONV_HALO:, :] = z_ref[...].astype(F32)

    cb = cb_ref[...]
    lng = lng_ref[...]
    lnb = lnb_ref[...]
    base = CONV_HALO - (CONV_K - 1)
    for c in range(tm // conv_rows):
        r0 = c * conv_rows
        acc = jnp.zeros((conv_rows, D_MODEL), F32)
        for kk in range(CONV_K):
            acc = acc + zext_ref[r0 + base + kk:r0 + base + kk + conv_rows, :] * cw_ref[kk:kk + 1, :]
        zc = acc + cb
        mu = jnp.mean(zc, axis=-1, keepdims=True)
        var = jnp.mean(jnp.square(zc - mu), axis=-1, keepdims=True)
        y = ((zc - mu) * lax.rsqrt(var + EPS)) * lng + lnb
        zc_ref[r0:r0 + conv_rows, :] = (y * jax.nn.sigmoid(y)).astype(BF16)

    y_attn = jnp.dot(o_ref[...], wo_ref[...], preferred_element_type=F32)
    y_conv = jnp.dot(zc_ref[...], wpw_ref[...], preferred_element_type=F32)
    mix = ga_ref[...].astype(F32) * y_attn + gc_ref[...].astype(F32) * y_conv
    h_res = x_ref[...] + jnp.dot(mix.astype(BF16), wout_ref[...], preferred_element_type=F32)
    hres_ref[...] = h_res

    ms = jnp.mean(h_res * h_res, axis=-1, keepdims=True)
    h2 = (h_res * lax.rsqrt(ms + EPS)) * n2g_ref[...]
    for s in range(ROW_TILES):
        h2_ref[pl.ds(s, tm, stride=ROW_TILES), :] = h2[:, s * LANES:(s + 1) * LANES]

    logits = jnp.dot(h2, wr_ref[...], preferred_element_type=F32, precision=lax.Precision.HIGHEST) + br_ref[...]
    lane = lax.broadcasted_iota(jnp.int32, (tm, LANES), 1)
    neg = -jnp.inf
    big = jnp.int32(1 << 20)
    gl = jnp.where((lane >= N_EXPERTS) & (lane < N_EXPERTS + N_GROUPS), logits, neg)
    gmax = jnp.max(gl, axis=-1, keepdims=True)
    gidx = jnp.min(jnp.where(gl == gmax, lane - N_EXPERTS, big), axis=-1, keepdims=True)
    g_w = 1.0 / jnp.sum(jnp.exp(gl - gmax), axis=-1, keepdims=True)
    el = jnp.where((lane < N_EXPERTS) & ((lane // EXPERTS_PER_GROUP) == gidx), logits, neg)
    e1 = jnp.max(el, axis=-1, keepdims=True)
    i1 = jnp.min(jnp.where(el == e1, lane, big), axis=-1, keepdims=True)
    el2 = jnp.where(lane == i1, neg, el)
    e2 = jnp.max(el2, axis=-1, keepdims=True)
    i2 = jnp.min(jnp.where(el2 == e2, lane, big), axis=-1, keepdims=True)
    esum = jnp.sum(jnp.exp(el - e1), axis=-1, keepdims=True)
    p1 = 1.0 / esum
    p2 = jnp.exp(e2 - e1) / esum
    w1 = p1 / (p1 + p2)
    w2 = p2 / (p1 + p2)
    route = jnp.where(lane == 0, g_w * w1, 0.0)
    route = jnp.where(lane == 1, g_w * w2, route)
    route = jnp.where(lane == 2, i1.astype(F32), route)
    route = jnp.where(lane == 3, i2.astype(F32), route)
    route_ref[...] = route


def _post(o, z, z_meta, ga, gc, x2d, wo, wpw, wout, cw, cb, lng, lnb, n2g, wr, br, seq):
    rows = x2d.shape[0]
    tm = min(256, seq)
    tiles_per_seq = seq // tm
    conv_rows = 32
    row_spec = pl.BlockSpec((tm, D_MODEL), lambda i: (i, 0))
    full = lambda shape: pl.BlockSpec(shape, lambda i: (0, 0))
    halo_blocks = tm // CONV_HALO
    body = functools.partial(_post_body, tm=tm, tiles_per_seq=tiles_per_seq, conv_rows=conv_rows)
    return pl.pallas_call(
        body,
        grid=(rows // tm,),
        in_specs=[row_spec, row_spec,
                  pl.BlockSpec((CONV_HALO, D_MODEL), lambda i: (jnp.maximum(i * halo_blocks - 1, 0), 0)),
                  full((N_META, D_MODEL)), row_spec, row_spec, row_spec,
                  full((D_MODEL, D_MODEL)), full((D_MODEL, D_MODEL)), full((D_MODEL, D_MODEL)),
                  full((CONV_HALO, D_MODEL)), full((1, D_MODEL)), full((1, D_MODEL)), full((1, D_MODEL)),
                  full((1, D_MODEL)), full((D_MODEL, LANES)), full((1, LANES))],
        out_specs=[row_spec,
                   pl.BlockSpec((tm * ROW_TILES, LANES), lambda i: (i, 0)),
                   pl.BlockSpec((tm, LANES), lambda i: (i, 0))],
        out_shape=[jax.ShapeDtypeStruct((rows, D_MODEL), F32),
                   jax.ShapeDtypeStruct((rows * ROW_TILES, LANES), F32),
                   jax.ShapeDtypeStruct((rows, LANES), F32)],
        scratch_shapes=[pltpu.VMEM((tm + CONV_HALO, D_MODEL), F32), pltpu.VMEM((tm, D_MODEL), BF16)],
        compiler_params=_cparams(("arbitrary",)),
        name="post",
    )(o, z, z, z_meta, ga, gc, x2d, wo, wpw, wout, cw, cb, lng, lnb, n2g, wr, br)


def _ffn_body(blk_e_ref, cnt_ref, tok_ref, dst_ref, h2_hbm, wg_ref, wu_ref, wd_ref, y_hbm,
              xg_ref, yb_ref, gsem, ssem):
    i = pl.program_id(0)
    cnt = cnt_ref[i]

    def gather_copy(r):
        src = pl.multiple_of(tok_ref[0, 0, r] * ROW_TILES, ROW_TILES)
        return pltpu.make_async_copy(h2_hbm.at[pl.ds(src, ROW_TILES), :],
                                     xg_ref.at[pl.ds(pl.multiple_of(r * ROW_TILES, ROW_TILES), ROW_TILES), :], gsem)

    def scatter_copy(r):
        dst = pl.multiple_of(dst_ref[0, 0, r] * ROW_TILES, ROW_TILES)
        return pltpu.make_async_copy(yb_ref.at[pl.ds(pl.multiple_of(r * ROW_TILES, ROW_TILES), ROW_TILES), :],
                                     y_hbm.at[pl.ds(dst, ROW_TILES), :], ssem)

    @pl.when(cnt > 0)
    def _():
        def start_g(r, c):
            gather_copy(r).start()
            return c

        def wait_g(r, c):
            gather_copy(r).wait()
            return c

        lax.fori_loop(0, MOE_BLOCK, start_g, 0)
        lax.fori_loop(0, MOE_BLOCK, wait_g, 0)

        xb = jnp.concatenate([xg_ref[pl.ds(s, MOE_BLOCK, stride=ROW_TILES), :] for s in range(ROW_TILES)],
                             axis=1).astype(BF16)
        g = jnp.dot(xb, wg_ref[0], preferred_element_type=F32)
        u = jnp.dot(xb, wu_ref[0], preferred_element_type=F32)
        hmid = ((g * jax.nn.sigmoid(g)) * u).astype(BF16)
        y = jnp.dot(hmid, wd_ref[0], preferred_element_type=F32)
        for s in range(ROW_TILES):
            yb_ref[pl.ds(s, MOE_BLOCK, stride=ROW_TILES), :] = y[:, s * LANES:(s + 1) * LANES]

        def start_s(r, c):
            scatter_copy(r).start()
            return c

        def wait_s(r, c):
            scatter_copy(r).wait()
            return c

        lax.fori_loop(0, cnt, start_s, 0)
        lax.fori_loop(0, cnt, wait_s, 0)


def _ffn(blk_e, cnt, tok, dst, h2_tiles, wg, wu, wd, n_slots):
    nb = blk_e.shape[0]
    smem_spec = pl.BlockSpec((1, 1, MOE_BLOCK), lambda i, be, ct: (i, 0, 0), memory_space=pltpu.SMEM)
    grid_spec = pltpu.PrefetchScalarGridSpec(
        num_scalar_prefetch=2,
        grid=(nb,),
        in_specs=[smem_spec, smem_spec,
                  pl.BlockSpec(memory_space=pl.ANY),
                  pl.BlockSpec((1, D_MODEL, D_EXPERT), lambda i, be, ct: (be[i], 0, 0)),
                  pl.BlockSpec((1, D_MODEL, D_EXPERT), lambda i, be, ct: (be[i], 0, 0)),
                  pl.BlockSpec((1, D_EXPERT, D_MODEL), lambda i, be, ct: (be[i], 0, 0))],
        out_specs=pl.BlockSpec(memory_space=pl.ANY),
        scratch_shapes=[pltpu.VMEM((MOE_BLOCK * ROW_TILES, LANES), F32),
                        pltpu.VMEM((MOE_BLOCK * ROW_TILES, LANES), F32),
                        pltpu.SemaphoreType.DMA(()), pltpu.SemaphoreType.DMA(())],
    )
    return pl.pallas_call(
        _ffn_body,
        grid_spec=grid_spec,
        out_shape=jax.ShapeDtypeStruct((n_slots * ROW_TILES, LANES), F32),
        compiler_params=_cparams(("arbitrary",)),
        name="ffn",
    )(blk_e, cnt, tok, dst, h2_tiles, wg, wu, wd)


def _dispatch(route, n_tok):
    eid = route[:, 2:2 + TOP_K].astype(jnp.int32)
    n_assign = n_tok * TOP_K
    flat_e = eid.reshape(n_assign)
    order = jnp.argsort(flat_e, stable=True).astype(jnp.int32)
    se = flat_e[order]
    bounds = jnp.searchsorted(se, jnp.arange(N_EXPERTS + 1, dtype=jnp.int32), side='left').astype(jnp.int32)
    starts = bounds[:-1]
    counts = bounds[1:] - bounds[:-1]
    pcounts = (counts + MOE_BLOCK - 1) // MOE_BLOCK * MOE_BLOCK
    pends = jnp.cumsum(pcounts)
    pstarts = pends - pcounts
    nb = -(-n_assign // MOE_BLOCK) + N_EXPERTS
    blk_start = jnp.arange(nb, dtype=jnp.int32) * MOE_BLOCK
    blk_e = jnp.minimum(jnp.searchsorted(pends, blk_start, side='right'), N_EXPERTS - 1).astype(jnp.int32)
    blk_cnt = jnp.clip(counts[blk_e] - (blk_start - pstarts[blk_e]), 0, MOE_BLOCK).astype(jnp.int32)
    within = jnp.arange(MOE_BLOCK, dtype=jnp.int32)[None, :]
    valid = within < blk_cnt[:, None]
    src = jnp.clip((starts[blk_e] + blk_start - pstarts[blk_e])[:, None] + within, 0, n_assign - 1)
    assign = order[src]
    tok = jnp.where(valid, assign // TOP_K, 0).astype(jnp.int32)
    dst = jnp.where(valid, (assign % TOP_K) * n_tok + assign // TOP_K, 0).astype(jnp.int32)
    return blk_e, blk_cnt, tok.reshape(nb, 1, MOE_BLOCK), dst.reshape(nb, 1, MOE_BLOCK)


def _final_body(hres_ref, y0_ref, y1_ref, route_ref, g_ref, out_ref, *, tm):
    y0 = jnp.concatenate([y0_ref[pl.ds(s, tm, stride=ROW_TILES), :] for s in range(ROW_TILES)], axis=1)
    y1 = jnp.concatenate([y1_ref[pl.ds(s, tm, stride=ROW_TILES), :] for s in range(ROW_TILES)], axis=1)
    route = route_ref[...]
    h = hres_ref[...] + (y0 * route[:, 0:1] + y1 * route[:, 1:2])
    ms = jnp.mean(h * h, axis=-1, keepdims=True)
    out_ref[...] = (h * lax.rsqrt(ms + EPS)) * g_ref[...]


def _final(h_res, y_tiles, route, final_g, seq):
    rows = h_res.shape[0]
    tm = min(256, seq)
    nt = rows // tm
    row_spec = pl.BlockSpec((tm, D_MODEL), lambda i: (i, 0))
    return pl.pallas_call(
        functools.partial(_final_body, tm=tm),
        grid=(nt,),
        in_specs=[row_spec,
                  pl.BlockSpec((tm * ROW_TILES, LANES), lambda i: (i, 0)),
                  pl.BlockSpec((tm * ROW_TILES, LANES), lambda i: (i + nt, 0)),
                  pl.BlockSpec((tm, LANES), lambda i: (i, 0)),
                  pl.BlockSpec((1, D_MODEL), lambda i: (0, 0))],
        out_specs=row_spec,
        out_shape=jax.ShapeDtypeStruct((rows, D_MODEL), F32),
        compiler_params=_cparams(("arbitrary",)),
        name="final",
    )(h_res, y_tiles, y_tiles, route, final_g)


def kernel(x, meta_tokens, norm1_g, w_in, lam_q1, lam_k1, lam_q2, lam_k2, subln_g, w_o_attn, conv_w, conv_b,
           conv_ln_g, conv_ln_b, w_pw2, w_out, norm2_g, w_group, b_group, w_router, b_router, w_e_gate,
           w_e_up, w_e_down, final_g):
    batch, seq, _ = x.shape
    n_tok = batch * seq
    layer = 0
    lam_init = 0.8 - 0.6 * math.exp(-0.3 * layer)
    row = lambda a: a.reshape(1, -1).astype(F32)

    x2d = x.reshape(n_tok, D_MODEL)
    w_in_b = w_in[layer].astype(BF16)
    rc, rlo, rhi = _rope_tables(N_META + seq)
    tm_in = min(512, seq)
    q, k, v, z, ga, gc = _inproj(x2d, row(norm1_g[layer]), w_in_b, rc[N_META:], rlo[N_META:], rhi[N_META:], tm_in)
    _, k_meta, v_meta, z_meta, _, _ = _inproj(meta_tokens.astype(F32), row(norm1_g[layer]), w_in_b,
                                              rc[:N_META], rlo[:N_META], rhi[:N_META], N_META)

    o = _attention(q, k, v, k_meta, v_meta, row(lam_q1[layer]), row(lam_k1[layer]), row(lam_q2[layer]),
                   row(lam_k2[layer]), row(subln_g[layer]), batch, seq, lam_init)

    cw = jnp.concatenate([conv_w[layer].astype(F32), jnp.zeros((CONV_HALO - CONV_K, D_MODEL), F32)], axis=0)
    pad_cols = LANES - N_EXPERTS - N_GROUPS
    wr = jnp.concatenate([w_router[layer].astype(F32), w_group[layer].astype(F32),
                          jnp.zeros((D_MODEL, pad_cols), F32)], axis=1)
    br = jnp.concatenate([b_router[layer].astype(F32), b_group[layer].astype(F32),
                          jnp.zeros((pad_cols,), F32)]).reshape(1, LANES)
    h_res, h2_tiles, route = _post(o, z, z_meta, ga, gc, x2d, w_o_attn[layer].astype(BF16),
                                   w_pw2[layer].astype(BF16), w_out[layer].astype(BF16), cw,
                                   row(conv_b[layer]), row(conv_ln_g[layer]), row(conv_ln_b[layer]),
                                   row(norm2_g[layer]), wr, br, seq)

    blk_e, blk_cnt, tok, dst = _dispatch(route, n_tok)
    y_tiles = _ffn(blk_e, blk_cnt, tok, dst, h2_tiles, w_e_gate[layer].astype(BF16), w_e_up[layer].astype(BF16),
                   w_e_down[layer].astype(BF16), TOP_K * n_tok)

    out = _final(h_res, y_tiles, route, row(final_g), seq)
    return out.reshape(batch, seq, D_MODEL)
```

```python
import functools
import math

import jax
import jax.numpy as jnp
import numpy as np
from jax import lax
from jax.experimental import pallas as pl
from jax.experimental.pallas import tpu as pltpu

F32 = jnp.float32
BF16 = jnp.bfloat16

D_MODEL = 1024
CHUNK = 64
N_META = 16
N_HEADS = 8
HEAD_DIM = 64
ROT_DIM = HEAD_DIM // 4
ROPE_THETA = 500000.0
ATTN_W = N_HEADS * 2 * HEAD_DIM
CONV_K = 31
N_GROUPS = 4
EXPERTS_PER_GROUP = 8
N_EXPERTS = N_GROUPS * EXPERTS_PER_GROUP
TOP_K = 2
D_EXPERT = D_MODEL // 2
MOE_BLOCK = 512
EPS = 1e-6
LANES = 128
SUBLANES = 8
N_PROJ = 7
CONV_HALO = 32
VMEM_LIMIT = 48 * 1024 * 1024


def _cparams(sem):
    return pltpu.CompilerParams(dimension_semantics=sem, vmem_limit_bytes=VMEM_LIMIT)


def _rope(t, c, s_lo, s_hi):
    half = ROT_DIM // 2
    n = t.shape[-1]
    return t * c + pltpu.roll(t, n - half, 1) * s_lo + pltpu.roll(t, half, 1) * s_hi


def _inproj_body(x_ref, g_ref, w_ref, rc_ref, rlo_ref, rhi_ref,
                 q_ref, k_ref, v_ref, z_ref, ga_ref, gc_ref):
    x = x_ref[...]
    ms = jnp.mean(x * x, axis=-1, keepdims=True)
    xn = ((x * lax.rsqrt(ms + EPS)) * g_ref[...]).astype(BF16)

    def proj(j):
        return jnp.dot(xn, w_ref[:, j * D_MODEL:(j + 1) * D_MODEL], preferred_element_type=F32)

    reps = (1, D_MODEL // LANES)
    c, lo, hi = jnp.tile(rc_ref[...], reps), jnp.tile(rlo_ref[...], reps), jnp.tile(rhi_ref[...], reps)
    q_ref[...] = (_rope(proj(0), c, lo, hi) * (HEAD_DIM ** -0.5)).astype(BF16)
    k_ref[...] = _rope(proj(1), c, lo, hi).astype(BF16)
    v_ref[...] = proj(2).astype(BF16)
    z_ref[...] = (proj(3) * jax.nn.sigmoid(proj(4))).astype(BF16)
    ga_ref[...] = jax.nn.sigmoid(proj(5)).astype(BF16)
    gc_ref[...] = jax.nn.sigmoid(proj(6)).astype(BF16)


def _inproj(x2d, g, w_bf16, rc, rlo, rhi, tm):
    rows = x2d.shape[0]
    tab_blocks = rc.shape[0] // tm
    row_spec = pl.BlockSpec((tm, D_MODEL), lambda i: (i, 0))
    tab_spec = pl.BlockSpec((tm, LANES), lambda i: (i % tab_blocks, 0))
    out = jax.ShapeDtypeStruct((rows, D_MODEL), BF16)
    return pl.pallas_call(
        _inproj_body,
        grid=(rows // tm,),
        in_specs=[row_spec,
                  pl.BlockSpec((1, D_MODEL), lambda i: (0, 0)),
                  pl.BlockSpec((D_MODEL, N_PROJ * D_MODEL), lambda i: (0, 0), pipeline_mode=pl.Buffered(1)),
                  tab_spec, tab_spec, tab_spec],
        out_specs=[row_spec] * 6,
        out_shape=[out] * 6,
        compiler_params=_cparams(("arbitrary",)),
        name="inproj",
    )(x2d, g, w_bf16, rc, rlo, rhi)


def _rope_tables(length):
    half = ROT_DIM // 2
    inv_freq = np.float32(ROPE_THETA) ** (-np.arange(0, ROT_DIM, 2, dtype=np.float32) / np.float32(ROT_DIM))
    ang = (np.arange(length, dtype=np.float32)[:, None] * inv_freq[None, :]).astype(np.float32)
    cos, sin = np.cos(ang).astype(np.float32), np.sin(ang).astype(np.float32)
    ones = np.ones((length, HEAD_DIM - ROT_DIM), np.float32)
    c = np.concatenate([cos, cos, ones], axis=1)
    lo = np.concatenate([-sin, np.zeros((length, HEAD_DIM - half), np.float32)], axis=1)
    hi = np.concatenate([np.zeros((length, half), np.float32), sin,
                         np.zeros((length, HEAD_DIM - ROT_DIM), np.float32)], axis=1)
    rep = LANES // HEAD_DIM
    return tuple(jnp.asarray(np.tile(t, (1, rep))) for t in (c, lo, hi))


def _dot_nt(a, b):
    return lax.dot_general(a, b, (((1,), (1,)), ((), ())), preferred_element_type=F32)


def _attn_body(lq1_ref, lk1_ref, lq2_ref, lk2_ref, sg_ref, q_ref, k_ref, v_ref, km_ref, vm_ref,
               o_ref, *, seq, tq, tk_max, lam_init):
    lam = (jnp.exp(jnp.sum(lq1_ref[...] * lk1_ref[...], keepdims=True))
           - jnp.exp(jnp.sum(lq2_ref[...] * lk2_ref[...], keepdims=True)) + lam_init)
    lane = lax.broadcasted_iota(jnp.int32, (tq, LANES), 1)
    lane2 = lax.broadcasted_iota(jnp.int32, (2 * tq, LANES), 1)
    neg = -jnp.inf
    km = km_ref[...]
    vm = vm_ref[...]

    for qi in range(seq // tq):
        q = q_ref[qi * tq:(qi + 1) * tq, :]
        zero = jnp.zeros_like(q)
        q2 = jnp.concatenate([jnp.where(lane < HEAD_DIM, q, zero), jnp.where(lane >= HEAD_DIM, q, zero)], axis=0)
        n_keys = (qi + 1) * tq
        m = l = acc = None
        start = 0
        while start < n_keys:
            width = min(tk_max, n_keys - start)
            s = _dot_nt(q2, k_ref[start:start + width, :])
            if start + width > qi * tq:
                row_chunk = (qi * tq + lax.broadcasted_iota(jnp.int32, s.shape, 0) % tq) // CHUNK
                col_chunk = (start + lax.broadcasted_iota(jnp.int32, s.shape, 1)) // CHUNK
                s = jnp.where(col_chunk <= row_chunk, s, neg)
            tiles = [s[:, t * LANES:(t + 1) * LANES] for t in range(width // LANES)]
            if start == 0:
                tiles.append(jnp.where(lane2 < N_META, _dot_nt(q2, km), neg))
            tile_max = functools.reduce(jnp.maximum, tiles)
            m_blk = jnp.broadcast_to(jnp.max(tile_max, axis=-1, keepdims=True), tile_max.shape)
            m_new = m_blk if m is None else jnp.maximum(m, m_blk)
            p_tiles = [jnp.exp(t - m_new) for t in tiles]
            n_real = width // LANES
            p_sum = functools.reduce(jnp.add, p_tiles)
            pv = jnp.dot(jnp.concatenate(p_tiles[:n_real], axis=1).astype(BF16), v_ref[start:start + width, :],
                         preferred_element_type=F32)
            if start == 0:
                pv = pv + jnp.dot(p_tiles[n_real].astype(BF16), vm, preferred_element_type=F32)
            if m is None:
                l, acc = p_sum, pv
            else:
                alpha = jnp.exp(m - m_new)
                l = alpha * l + p_sum
                acc = alpha * acc + pv
            m = m_new
            start += width

        o_maps = acc / jnp.sum(l, axis=-1, keepdims=True)
        o = o_maps[:tq] - lam * o_maps[tq:]
        ms = jnp.mean(o * o, axis=-1, keepdims=True)
        o = ((o * lax.rsqrt(ms + EPS)) * sg_ref[...]) * (1.0 - lam_init)
        o_ref[qi * tq:(qi + 1) * tq, :] = o.astype(BF16)


def _attention(q, k, v, k_meta, v_meta, lq1, lk1, lq2, lk2, subln_g, batch, seq, lam_init):
    tq = min(256, seq)
    head_spec = pl.BlockSpec((seq, LANES), lambda b, h: (b, h))
    meta_spec = pl.BlockSpec((LANES, LANES), lambda b, h: (0, h))
    lam_spec = pl.BlockSpec((1, HEAD_DIM), lambda b, h: (0, 0))
    pad = jnp.zeros((LANES - N_META, k_meta.shape[1]), k_meta.dtype)
    k_meta = jnp.concatenate([k_meta, pad], axis=0)
    v_meta = jnp.concatenate([v_meta, pad], axis=0)
    body = functools.partial(_attn_body, seq=seq, tq=tq, tk_max=512, lam_init=lam_init)
    return pl.pallas_call(
        body,
        grid=(batch, N_HEADS),
        in_specs=[lam_spec] * 4 + [pl.BlockSpec((1, LANES), lambda b, h: (0, 0)),
                                   head_spec, head_spec, head_spec, meta_spec, meta_spec],
        out_specs=head_spec,
        out_shape=jax.ShapeDtypeStruct(q.shape, BF16),
        compiler_params=_cparams(("arbitrary", "arbitrary")),
        name="attn",
    )(lq1, lk1, lq2, lk2, subln_g, q, k, v, k_meta, v_meta)


def _post_body(o_ref, z_ref, zprev_ref, zmeta_ref, ga_ref, gc_ref, x_ref,
               wo_ref, wpw_ref, wout_ref, cw_ref, cb_ref, lng_ref, lnb_ref, n2g_ref, wr_ref, br_ref,
               hres_ref, h2_ref, route_ref, zext_ref, zph_ref, zc_ref, *, tm, tiles_per_seq, conv_rows):
    i = pl.program_id(0)
    first = (i % tiles_per_seq) == 0
    meta_halo = jnp.concatenate(
        [jnp.zeros((CONV_HALO - N_META, D_MODEL), F32), zmeta_ref[...].astype(F32)], axis=0)
    zext_ref[0:CONV_HALO, :] = jnp.where(first, meta_halo, zprev_ref[...].astype(F32))
    zext_ref[CONV_HALO:, :] = z_ref[...].astype(F32)
    ph_rows = zph_ref.shape[1]
    for b in range(1, SUBLANES):
        zph_ref[b - 1] = zext_ref[b:b + ph_rows, :]

    cb = cb_ref[...]
    lng = lng_ref[...]
    lnb = lnb_ref[...]
    base = CONV_HALO - (CONV_K - 1)
    for c in range(tm // conv_rows):
        r0 = c * conv_rows
        acc = jnp.zeros((conv_rows, D_MODEL), F32)
        for kk in range(CONV_K):
            a, b = divmod(base + kk, SUBLANES)
            lo = r0 + a * SUBLANES
            win = zext_ref[lo:lo + conv_rows, :] if b == 0 else zph_ref[b - 1, lo:lo + conv_rows, :]
            acc = acc + win * cw_ref[kk:kk + 1, :]
        zc = acc + cb
        mu = jnp.mean(zc, axis=-1, keepdims=True)
        var = jnp.mean(jnp.square(zc - mu), axis=-1, keepdims=True)
        y = ((zc - mu) * lax.rsqrt(var + EPS)) * lng + lnb
        zc_ref[r0:r0 + conv_rows, :] = (y * jax.nn.sigmoid(y)).astype(BF16)

    y_attn = jnp.dot(o_ref[...], wo_ref[...], preferred_element_type=F32)
    y_conv = jnp.dot(zc_ref[...], wpw_ref[...], preferred_element_type=F32)
    mix = ga_ref[...].astype(F32) * y_attn + gc_ref[...].astype(F32) * y_conv
    h_res = x_ref[...] + jnp.dot(mix.astype(BF16), wout_ref[...], preferred_element_type=F32)
    hres_ref[...] = h_res

    ms = jnp.mean(h_res * h_res, axis=-1, keepdims=True)
    h2 = (h_res * lax.rsqrt(ms + EPS)) * n2g_ref[...]
    h2_ref[...] = h2

    logits = jnp.dot(h2, wr_ref[...], preferred_element_type=F32, precision=lax.Precision.HIGHEST) + br_ref[...]
    lane = lax.broadcasted_iota(jnp.int32, (tm, LANES), 1)
    neg = -jnp.inf
    big = jnp.int32(1 << 20)
    gl = jnp.where((lane >= N_EXPERTS) & (lane < N_EXPERTS + N_GROUPS), logits, neg)
    gmax = jnp.max(gl, axis=-1, keepdims=True)
    gidx = jnp.min(jnp.where(gl == gmax, lane - N_EXPERTS, big), axis=-1, keepdims=True)
    g_w = 1.0 / jnp.sum(jnp.exp(gl - gmax), axis=-1, keepdims=True)
    el = jnp.where((lane < N_EXPERTS) & ((lane // EXPERTS_PER_GROUP) == gidx), logits, neg)
    e1 = jnp.max(el, axis=-1, keepdims=True)
    i1 = jnp.min(jnp.where(el == e1, lane, big), axis=-1, keepdims=True)
    el2 = jnp.where(lane == i1, neg, el)
    e2 = jnp.max(el2, axis=-1, keepdims=True)
    i2 = jnp.min(jnp.where(el2 == e2, lane, big), axis=-1, keepdims=True)
    esum = jnp.sum(jnp.exp(el - e1), axis=-1, keepdims=True)
    p1 = 1.0 / esum
    p2 = jnp.exp(e2 - e1) / esum
    w1 = p1 / (p1 + p2)
    w2 = p2 / (p1 + p2)
    route = jnp.where(lane == 0, g_w * w1, 0.0)
    route = jnp.where(lane == 1, g_w * w2, route)
    route = jnp.where(lane == 2, i1.astype(F32), route)
    route = jnp.where(lane == 3, i2.astype(F32), route)
    route_ref[...] = route


def _post(o, z, z_meta, ga, gc, x2d, wo, wpw, wout, cw, cb, lng, lnb, n2g, wr, br, seq):
    rows = x2d.shape[0]
    tm = min(256, seq)
    tiles_per_seq = seq // tm
    conv_rows = 32
    row_spec = pl.BlockSpec((tm, D_MODEL), lambda i: (i, 0))
    full = lambda shape: pl.BlockSpec(shape, lambda i: (0, 0))
    halo_blocks = tm // CONV_HALO
    body = functools.partial(_post_body, tm=tm, tiles_per_seq=tiles_per_seq, conv_rows=conv_rows)
    return pl.pallas_call(
        body,
        grid=(rows // tm,),
        in_specs=[row_spec, row_spec,
                  pl.BlockSpec((CONV_HALO, D_MODEL), lambda i: (jnp.maximum(i * halo_blocks - 1, 0), 0)),
                  full((N_META, D_MODEL)), row_spec, row_spec, row_spec,
                  full((D_MODEL, D_MODEL)), full((D_MODEL, D_MODEL)), full((D_MODEL, D_MODEL)),
                  full((CONV_HALO, D_MODEL)), full((1, D_MODEL)), full((1, D_MODEL)), full((1, D_MODEL)),
                  full((1, D_MODEL)), full((D_MODEL, LANES)), full((1, LANES))],
        out_specs=[row_spec, row_spec, pl.BlockSpec((tm, LANES), lambda i: (i, 0))],
        out_shape=[jax.ShapeDtypeStruct((rows, D_MODEL), F32),
                   jax.ShapeDtypeStruct((rows, D_MODEL), F32),
                   jax.ShapeDtypeStruct((rows, LANES), F32)],
        scratch_shapes=[pltpu.VMEM((tm + CONV_HALO, D_MODEL), F32),
                        pltpu.VMEM((SUBLANES - 1, tm + CONV_HALO - SUBLANES, D_MODEL), F32),
                        pltpu.VMEM((tm, D_MODEL), BF16)],
        compiler_params=_cparams(("arbitrary",)),
        name="post",
    )(o, z, z, z_meta, ga, gc, x2d, wo, wpw, wout, cw, cb, lng, lnb, n2g, wr, br)


ROW_UNROLL = 8


def _ffn_body(blk_e_ref, nact_ref, tok_ref, tok_next_ref, dst_ref, h2_hbm, wg_ref, wu_ref, wd_ref, y_hbm,
              xg_ref, yb_ref, gsem, ssem):
    i = pl.program_id(0)
    nact = nact_ref[0]
    slot = i % 2

    def start_gather(t_ref, sl):
        def body(r, c):
            pltpu.make_async_copy(h2_hbm.at[pl.ds(t_ref[0, 0, r], 1), :], xg_ref.at[sl, pl.ds(r, 1), :],
                                  gsem.at[sl]).start()
            return c
        lax.fori_loop(0, MOE_BLOCK, body, 0, unroll=ROW_UNROLL)

    def wait_gather(sl):
        pltpu.make_async_copy(h2_hbm.at[pl.ds(0, MOE_BLOCK), :], xg_ref.at[sl], gsem.at[sl]).wait()

    def start_scatter():
        def body(r, c):
            pltpu.make_async_copy(yb_ref.at[pl.ds(r, 1), :], y_hbm.at[pl.ds(dst_ref[0, 0, r], 1), :], ssem).start()
            return c
        lax.fori_loop(0, MOE_BLOCK, body, 0, unroll=ROW_UNROLL)

    def wait_scatter():
        pltpu.make_async_copy(yb_ref, y_hbm.at[pl.ds(0, MOE_BLOCK), :], ssem).wait()

    @pl.when(i == 0)
    def _():
        yb_ref[...] = jnp.zeros_like(yb_ref)
        spare = pltpu.make_async_copy(yb_ref, y_hbm.at[pl.ds(y_hbm.shape[0] - MOE_BLOCK, MOE_BLOCK), :], ssem)
        spare.start()
        spare.wait()

    @pl.when((i == 0) & (nact > 0))
    def _():
        start_gather(tok_ref, 0)

    @pl.when(i < nact)
    def _():
        wait_gather(slot)

        @pl.when(i + 1 < nact)
        def _():
            start_gather(tok_next_ref, 1 - slot)

        xb = xg_ref[slot].astype(BF16)
        g = jnp.dot(xb, wg_ref[0], preferred_element_type=F32)
        u = jnp.dot(xb, wu_ref[0], preferred_element_type=F32)
        hmid = ((g * jax.nn.sigmoid(g)) * u).astype(BF16)
        y = jnp.dot(hmid, wd_ref[0], preferred_element_type=F32)

        @pl.when(i > 0)
        def _():
            wait_scatter()

        yb_ref[...] = y
        start_scatter()

        @pl.when(i == nact - 1)
        def _():
            wait_scatter()


def _ffn(blk_e, nact, tok, dst, h2, wg, wu, wd, n_out_rows):
    nb = blk_e.shape[0]
    smem_block = (1, 1, MOE_BLOCK)
    cur = pl.BlockSpec(smem_block, lambda i, be, na: (i, 0, 0), memory_space=pltpu.SMEM)
    nxt = pl.BlockSpec(smem_block, lambda i, be, na: (jnp.minimum(i + 1, nb - 1), 0, 0), memory_space=pltpu.SMEM)
    grid_spec = pltpu.PrefetchScalarGridSpec(
        num_scalar_prefetch=2,
        grid=(nb,),
        in_specs=[cur, nxt, cur,
                  pl.BlockSpec(memory_space=pl.ANY),
                  pl.BlockSpec((1, D_MODEL, D_EXPERT), lambda i, be, na: (be[i], 0, 0)),
                  pl.BlockSpec((1, D_MODEL, D_EXPERT), lambda i, be, na: (be[i], 0, 0)),
                  pl.BlockSpec((1, D_EXPERT, D_MODEL), lambda i, be, na: (be[i], 0, 0))],
        out_specs=pl.BlockSpec(memory_space=pl.ANY),
        scratch_shapes=[pltpu.VMEM((2, MOE_BLOCK, D_MODEL), F32),
                        pltpu.VMEM((MOE_BLOCK, D_MODEL), F32),
                        pltpu.SemaphoreType.DMA((2,)), pltpu.SemaphoreType.DMA(())],
    )
    return pl.pallas_call(
        _ffn_body,
        grid_spec=grid_spec,
        out_shape=jax.ShapeDtypeStruct((n_out_rows, D_MODEL), F32),
        compiler_params=_cparams(("arbitrary",)),
        name="ffn",
    )(blk_e, nact, tok, tok, dst, h2, wg, wu, wd)


def _dispatch(route, n_tok):
    eid = route[:, 2:2 + TOP_K].astype(jnp.int32)
    n_assign = n_tok * TOP_K
    flat_e = eid.reshape(n_assign)
    order = jnp.argsort(flat_e, stable=True).astype(jnp.int32)
    experts = jnp.arange(N_EXPERTS, dtype=jnp.int32)
    counts = jnp.sum((flat_e[None, :] == experts[:, None]).astype(jnp.int32), axis=1)
    starts = jnp.cumsum(counts) - counts
    pcounts = (counts + MOE_BLOCK - 1) // MOE_BLOCK * MOE_BLOCK
    pends = jnp.cumsum(pcounts)
    pstarts = pends - pcounts
    nb = -(-n_assign // MOE_BLOCK) + N_EXPERTS
    blk_start = jnp.arange(nb, dtype=jnp.int32) * MOE_BLOCK
    blk_e = jnp.minimum(jnp.sum((pends[None, :] <= blk_start[:, None]).astype(jnp.int32), axis=1), N_EXPERTS - 1)
    blk_cnt = jnp.clip(counts[blk_e] - (blk_start - pstarts[blk_e]), 0, MOE_BLOCK).astype(jnp.int32)
    within = jnp.arange(MOE_BLOCK, dtype=jnp.int32)[None, :]
    valid = within < blk_cnt[:, None]
    src = jnp.clip((starts[blk_e] + blk_start - pstarts[blk_e])[:, None] + within, 0, n_assign - 1)
    assign = order[src]
    tok = jnp.where(valid, assign // TOP_K, 0).astype(jnp.int32)
    dst = jnp.where(valid, (assign % TOP_K) * n_tok + assign // TOP_K, TOP_K * n_tok + within).astype(jnp.int32)
    n_active = (pends[-1:] // MOE_BLOCK).astype(jnp.int32)
    return blk_e, n_active, tok.reshape(nb, 1, MOE_BLOCK), dst.reshape(nb, 1, MOE_BLOCK)


def _final_body(hres_ref, y0_ref, y1_ref, route_ref, g_ref, out_ref):
    route = route_ref[...]
    h = hres_ref[...] + (y0_ref[...] * route[:, 0:1] + y1_ref[...] * route[:, 1:2])
    ms = jnp.mean(h * h, axis=-1, keepdims=True)
    out_ref[...] = (h * lax.rsqrt(ms + EPS)) * g_ref[...]


def _final(h_res, y_rows, route, final_g, seq):
    rows = h_res.shape[0]
    tm = min(512, seq)
    nt = rows // tm
    row_spec = pl.BlockSpec((tm, D_MODEL), lambda i: (i, 0))
    return pl.pallas_call(
        _final_body,
        grid=(nt,),
        in_specs=[row_spec, row_spec,
                  pl.BlockSpec((tm, D_MODEL), lambda i: (i + nt, 0)),
                  pl.BlockSpec((tm, LANES), lambda i: (i, 0)),
                  pl.BlockSpec((1, D_MODEL), lambda i: (0, 0))],
        out_specs=row_spec,
        out_shape=jax.ShapeDtypeStruct((rows, D_MODEL), F32),
        compiler_params=_cparams(("arbitrary",)),
        name="final",
    )(h_res, y_rows, y_rows, route, final_g)


def kernel(x, meta_tokens, norm1_g, w_in, lam_q1, lam_k1, lam_q2, lam_k2, subln_g, w_o_attn, conv_w, conv_b,
           conv_ln_g, conv_ln_b, w_pw2, w_out, norm2_g, w_group, b_group, w_router, b_router, w_e_gate,
           w_e_up, w_e_down, final_g):
    batch, seq, _ = x.shape
    n_tok = batch * seq
    layer = 0
    lam_init = 0.8 - 0.6 * math.exp(-0.3 * layer)
    row = lambda a: a.reshape(1, -1).astype(F32)

    x2d = x.reshape(n_tok, D_MODEL)
    w_in_b = w_in[layer].astype(BF16)
    rc, rlo, rhi = _rope_tables(N_META + seq)
    tm_in = min(512, seq)
    q, k, v, z, ga, gc = _inproj(x2d, row(norm1_g[layer]), w_in_b, rc[N_META:], rlo[N_META:], rhi[N_META:], tm_in)
    _, k_meta, v_meta, z_meta, _, _ = _inproj(meta_tokens.astype(F32), row(norm1_g[layer]), w_in_b,
                                              rc[:N_META], rlo[:N_META], rhi[:N_META], N_META)

    o = _attention(q, k, v, k_meta, v_meta, row(lam_q1[layer]), row(lam_k1[layer]), row(lam_q2[layer]),
                   row(lam_k2[layer]), row(subln_g[layer]), batch, seq, lam_init)

    cw = jnp.concatenate([conv_w[layer].astype(F32), jnp.zeros((CONV_HALO - CONV_K, D_MODEL), F32)], axis=0)
    pad_cols = LANES - N_EXPERTS - N_GROUPS
    wr = jnp.concatenate([w_router[layer].astype(F32), w_group[layer].astype(F32),
                          jnp.zeros((D_MODEL, pad_cols), F32)], axis=1)
    br = jnp.concatenate([b_router[layer].astype(F32), b_group[layer].astype(F32),
                          jnp.zeros((pad_cols,), F32)]).reshape(1, LANES)
    h_res, h2, route = _post(o, z, z_meta, ga, gc, x2d, w_o_attn[layer].astype(BF16),
                                   w_pw2[layer].astype(BF16), w_out[layer].astype(BF16), cw,
                                   row(conv_b[layer]), row(conv_ln_g[layer]), row(conv_ln_b[layer]),
                                   row(norm2_g[layer]), wr, br, seq)

    blk_e, n_active, tok, dst = _dispatch(route, n_tok)
    y_rows = _ffn(blk_e, n_active, tok, dst, h2, w_e_gate[layer].astype(BF16), w_e_up[layer].astype(BF16),
                  w_e_down[layer].astype(BF16), TOP_K * n_tok + MOE_BLOCK)

    out = _final(h_res, y_rows, route, row(final_g), seq)
    return out.reshape(batch, seq, D_MODEL)
```

```python
import functools
import math

import jax
import jax.numpy as jnp
import numpy as np
from jax import lax
from jax.experimental import pallas as pl
from jax.experimental.pallas import tpu as pltpu

F32 = jnp.float32
BF16 = jnp.bfloat16

D_MODEL = 1024
CHUNK = 64
N_META = 16
N_HEADS = 8
HEAD_DIM = 64
ROT_DIM = HEAD_DIM // 4
ROPE_THETA = 500000.0
ATTN_W = N_HEADS * 2 * HEAD_DIM
CONV_K = 31
N_GROUPS = 4
EXPERTS_PER_GROUP = 8
N_EXPERTS = N_GROUPS * EXPERTS_PER_GROUP
TOP_K = 2
D_EXPERT = D_MODEL // 2
MOE_BLOCK = 512
EPS = 1e-6
LANES = 128
SUBLANES = 8
ROW_TILES = D_MODEL // LANES
N_PROJ = 7
CONV_HALO = 32
VMEM_LIMIT = 48 * 1024 * 1024


def _cparams(sem):
    return pltpu.CompilerParams(dimension_semantics=sem, vmem_limit_bytes=VMEM_LIMIT)


def _rope(t, c, s_lo, s_hi):
    half = ROT_DIM // 2
    n = t.shape[-1]
    return t * c + pltpu.roll(t, n - half, 1) * s_lo + pltpu.roll(t, half, 1) * s_hi


def _inproj_body(x_ref, g_ref, w_ref, rc_ref, rlo_ref, rhi_ref,
                 q_ref, k_ref, v_ref, z_ref, ga_ref, gc_ref):
    x = x_ref[...]
    ms = jnp.mean(x * x, axis=-1, keepdims=True)
    xn = ((x * lax.rsqrt(ms + EPS)) * g_ref[...]).astype(BF16)

    def proj(j):
        return jnp.dot(xn, w_ref[:, j * D_MODEL:(j + 1) * D_MODEL], preferred_element_type=F32)

    reps = (1, D_MODEL // LANES)
    c, lo, hi = jnp.tile(rc_ref[...], reps), jnp.tile(rlo_ref[...], reps), jnp.tile(rhi_ref[...], reps)
    q_ref[...] = (_rope(proj(0), c, lo, hi) * (HEAD_DIM ** -0.5 * math.log2(math.e))).astype(BF16)
    k_ref[...] = _rope(proj(1), c, lo, hi).astype(BF16)
    v_ref[...] = proj(2).astype(BF16)
    z_ref[...] = (proj(3) * jax.nn.sigmoid(proj(4))).astype(BF16)
    ga_ref[...] = jax.nn.sigmoid(proj(5)).astype(BF16)
    gc_ref[...] = jax.nn.sigmoid(proj(6)).astype(BF16)


def _inproj(x2d, g, w_bf16, rc, rlo, rhi, tm):
    rows = x2d.shape[0]
    tab_blocks = rc.shape[0] // tm
    row_spec = pl.BlockSpec((tm, D_MODEL), lambda i: (i, 0))
    tab_spec = pl.BlockSpec((tm, LANES), lambda i: (i % tab_blocks, 0))
    out = jax.ShapeDtypeStruct((rows, D_MODEL), BF16)
    return pl.pallas_call(
        _inproj_body,
        grid=(rows // tm,),
        in_specs=[row_spec,
                  pl.BlockSpec((1, D_MODEL), lambda i: (0, 0)),
                  pl.BlockSpec((D_MODEL, N_PROJ * D_MODEL), lambda i: (0, 0), pipeline_mode=pl.Buffered(1)),
                  tab_spec, tab_spec, tab_spec],
        out_specs=[row_spec] * 6,
        out_shape=[out] * 6,
        compiler_params=_cparams(("arbitrary",)),
        name="inproj",
    )(x2d, g, w_bf16, rc, rlo, rhi)


def _rope_tables(length):
    half = ROT_DIM // 2
    inv_freq = np.float32(ROPE_THETA) ** (-np.arange(0, ROT_DIM, 2, dtype=np.float32) / np.float32(ROT_DIM))
    ang = (np.arange(length, dtype=np.float32)[:, None] * inv_freq[None, :]).astype(np.float32)
    cos, sin = np.cos(ang).astype(np.float32), np.sin(ang).astype(np.float32)
    ones = np.ones((length, HEAD_DIM - ROT_DIM), np.float32)
    c = np.concatenate([cos, cos, ones], axis=1)
    lo = np.concatenate([-sin, np.zeros((length, HEAD_DIM - half), np.float32)], axis=1)
    hi = np.concatenate([np.zeros((length, half), np.float32), sin,
                         np.zeros((length, HEAD_DIM - ROT_DIM), np.float32)], axis=1)
    rep = LANES // HEAD_DIM
    return tuple(jnp.asarray(np.tile(t, (1, rep))) for t in (c, lo, hi))


def _dot_nt(a, b):
    return lax.dot_general(a, b, (((1,), (1,)), ((), ())), preferred_element_type=F32)


def _attn_body(lq1_ref, lk1_ref, lq2_ref, lk2_ref, sg_ref, q_ref, k_ref, v_ref, km_ref, vm_ref,
               o_ref, *, seq, tq, tk_max, lam_init):
    lam = (jnp.exp(jnp.sum(lq1_ref[...] * lk1_ref[...], keepdims=True))
           - jnp.exp(jnp.sum(lq2_ref[...] * lk2_ref[...], keepdims=True)) + lam_init)
    lane = lax.broadcasted_iota(jnp.int32, (tq, LANES), 1)
    lane2 = lax.broadcasted_iota(jnp.int32, (2 * tq, LANES), 1)
    neg = -jnp.inf
    km = km_ref[...]
    vm = vm_ref[...]

    for qi in range(seq // tq):
        q = q_ref[qi * tq:(qi + 1) * tq, :]
        zero = jnp.zeros_like(q)
        q2 = jnp.concatenate([jnp.where(lane < HEAD_DIM, q, zero), jnp.where(lane >= HEAD_DIM, q, zero)], axis=0)
        n_keys = (qi + 1) * tq
        blocks = [(start, min(tk_max, n_keys - start)) for start in range(0, n_keys, tk_max)]

        def scores(start, width):
            s = _dot_nt(q2, k_ref[start:start + width, :])
            if start + width > qi * tq:
                row_chunk = (qi * tq + lax.broadcasted_iota(jnp.int32, s.shape, 0) % tq) // CHUNK
                col_chunk = (start + lax.broadcasted_iota(jnp.int32, s.shape, 1)) // CHUNK
                s = jnp.where(col_chunk <= row_chunk, s, neg)
            tiles = [s[:, t * LANES:(t + 1) * LANES] for t in range(width // LANES)]
            if start == 0:
                tiles.append(jnp.where(lane2 < N_META, _dot_nt(q2, km), neg))
            return tiles

        m = l = acc = None
        tiles_next = scores(*blocks[0])
        for bi, (start, width) in enumerate(blocks):
            tiles = tiles_next
            if bi + 1 < len(blocks):
                tiles_next = scores(*blocks[bi + 1])
            tile_max = functools.reduce(jnp.maximum, tiles)
            m_blk = jnp.broadcast_to(jnp.max(tile_max, axis=-1, keepdims=True), tile_max.shape)
            m_new = m_blk if m is None else jnp.maximum(m, m_blk)
            p_tiles = [jnp.exp2(t - m_new) for t in tiles]
            n_real = width // LANES
            p_sum = functools.reduce(jnp.add, p_tiles)
            pv = jnp.dot(jnp.concatenate(p_tiles[:n_real], axis=1).astype(BF16), v_ref[start:start + width, :],
                         preferred_element_type=F32)
            if start == 0:
                pv = pv + jnp.dot(p_tiles[n_real].astype(BF16), vm, preferred_element_type=F32)
            if m is None:
                l, acc = p_sum, pv
            else:
                alpha = jnp.exp2(m - m_new)
                l = alpha * l + p_sum
                acc = alpha * acc + pv
            m = m_new

        o_maps = acc / jnp.sum(l, axis=-1, keepdims=True)
        o = o_maps[:tq] - lam * o_maps[tq:]
        ms = jnp.mean(o * o, axis=-1, keepdims=True)
        o = ((o * lax.rsqrt(ms + EPS)) * sg_ref[...]) * (1.0 - lam_init)
        o_ref[qi * tq:(qi + 1) * tq, :] = o.astype(BF16)


def _attention(q, k, v, k_meta, v_meta, lq1, lk1, lq2, lk2, subln_g, batch, seq, lam_init):
    tq = min(256, seq)
    head_spec = pl.BlockSpec((seq, LANES), lambda b, h: (b, h))
    meta_spec = pl.BlockSpec((LANES, LANES), lambda b, h: (0, h))
    lam_spec = pl.BlockSpec((1, HEAD_DIM), lambda b, h: (0, 0))
    pad = jnp.zeros((LANES - N_META, k_meta.shape[1]), k_meta.dtype)
    k_meta = jnp.concatenate([k_meta, pad], axis=0)
    v_meta = jnp.concatenate([v_meta, pad], axis=0)
    body = functools.partial(_attn_body, seq=seq, tq=tq, tk_max=512, lam_init=lam_init)
    return pl.pallas_call(
        body,
        grid=(batch, N_HEADS),
        in_specs=[lam_spec] * 4 + [pl.BlockSpec((1, LANES), lambda b, h: (0, 0)),
                                   head_spec, head_spec, head_spec, meta_spec, meta_spec],
        out_specs=head_spec,
        out_shape=jax.ShapeDtypeStruct(q.shape, BF16),
        compiler_params=_cparams(("arbitrary", "arbitrary")),
        name="attn",
    )(lq1, lk1, lq2, lk2, subln_g, q, k, v, k_meta, v_meta)


def _post_body(o_ref, z_ref, zprev_ref, zmeta_ref, ga_ref, gc_ref, x_ref,
               wo_ref, wpw_ref, wout_ref, cw_ref, cb_ref, lng_ref, lnb_ref, n2g_ref, wr_ref, br_ref,
               hres_ref, h2_ref, route_ref, zext_ref, zph_ref, zc_ref, *, tm, tiles_per_seq, conv_rows):
    i = pl.program_id(0)
    first = (i % tiles_per_seq) == 0
    meta_halo = jnp.concatenate(
        [jnp.zeros((CONV_HALO - N_META, D_MODEL), F32), zmeta_ref[...].astype(F32)], axis=0)
    zext_ref[0:CONV_HALO, :] = jnp.where(first, meta_halo, zprev_ref[...].astype(F32))
    zext_ref[CONV_HALO:, :] = z_ref[...].astype(F32)
    ph_rows = zph_ref.shape[1]
    for b in range(1, SUBLANES):
        zph_ref[b - 1] = zext_ref[b:b + ph_rows, :]

    cb = cb_ref[...]
    lng = lng_ref[...]
    lnb = lnb_ref[...]
    base = CONV_HALO - (CONV_K - 1)
    for c in range(tm // conv_rows):
        r0 = c * conv_rows
        acc = jnp.zeros((conv_rows, D_MODEL), F32)
        for kk in range(CONV_K):
            a, b = divmod(base + kk, SUBLANES)
            lo = r0 + a * SUBLANES
            win = zext_ref[lo:lo + conv_rows, :] if b == 0 else zph_ref[b - 1, lo:lo + conv_rows, :]
            acc = acc + win * cw_ref[kk:kk + 1, :]
        zc = acc + cb
        mu = jnp.mean(zc, axis=-1, keepdims=True)
        var = jnp.mean(jnp.square(zc - mu), axis=-1, keepdims=True)
        y = ((zc - mu) * lax.rsqrt(var + EPS)) * lng + lnb
        zc_ref[r0:r0 + conv_rows, :] = (y * jax.nn.sigmoid(y)).astype(BF16)

    y_attn = jnp.dot(o_ref[...], wo_ref[...], preferred_element_type=F32)
    y_conv = jnp.dot(zc_ref[...], wpw_ref[...], preferred_element_type=F32)
    mix = ga_ref[...].astype(F32) * y_attn + gc_ref[...].astype(F32) * y_conv
    h_res = x_ref[...] + jnp.dot(mix.astype(BF16), wout_ref[...], preferred_element_type=F32)
    hres_ref[...] = h_res

    ms = jnp.mean(h_res * h_res, axis=-1, keepdims=True)
    h2 = (h_res * lax.rsqrt(ms + EPS)) * n2g_ref[...]
    for s in range(ROW_TILES):
        h2_ref[pl.ds(s, tm, stride=ROW_TILES), :] = h2[:, s * LANES:(s + 1) * LANES]

    logits = jnp.dot(h2, wr_ref[...], preferred_element_type=F32, precision=lax.Precision.HIGHEST) + br_ref[...]
    lane = lax.broadcasted_iota(jnp.int32, (tm, LANES), 1)
    neg = -jnp.inf
    big = jnp.int32(1 << 20)
    gl = jnp.where((lane >= N_EXPERTS) & (lane < N_EXPERTS + N_GROUPS), logits, neg)
    gmax = jnp.max(gl, axis=-1, keepdims=True)
    gidx = jnp.min(jnp.where(gl == gmax, lane - N_EXPERTS, big), axis=-1, keepdims=True)
    g_w = 1.0 / jnp.sum(jnp.exp(gl - gmax), axis=-1, keepdims=True)
    el = jnp.where((lane < N_EXPERTS) & ((lane // EXPERTS_PER_GROUP) == gidx), logits, neg)
    e1 = jnp.max(el, axis=-1, keepdims=True)
    i1 = jnp.min(jnp.where(el == e1, lane, big), axis=-1, keepdims=True)
    el2 = jnp.where(lane == i1, neg, el)
    e2 = jnp.max(el2, axis=-1, keepdims=True)
    i2 = jnp.min(jnp.where(el2 == e2, lane, big), axis=-1, keepdims=True)
    esum = jnp.sum(jnp.exp(el - e1), axis=-1, keepdims=True)
    p1 = 1.0 / esum
    p2 = jnp.exp(e2 - e1) / esum
    w1 = p1 / (p1 + p2)
    w2 = p2 / (p1 + p2)
    route = jnp.where(lane == 0, g_w * w1, 0.0)
    route = jnp.where(lane == 1, g_w * w2, route)
    route = jnp.where(lane == 2, i1.astype(F32), route)
    route = jnp.where(lane == 3, i2.astype(F32), route)
    route_ref[...] = route


def _post(o, z, z_meta, ga, gc, x2d, wo, wpw, wout, cw, cb, lng, lnb, n2g, wr, br, seq):
    rows = x2d.shape[0]
    tm = min(256, seq)
    tiles_per_seq = seq // tm
    conv_rows = 32
    row_spec = pl.BlockSpec((tm, D_MODEL), lambda i: (i, 0))
    full = lambda shape: pl.BlockSpec(shape, lambda i: (0, 0))
    halo_blocks = tm // CONV_HALO
    body = functools.partial(_post_body, tm=tm, tiles_per_seq=tiles_per_seq, conv_rows=conv_rows)
    return pl.pallas_call(
        body,
        grid=(rows // tm,),
        in_specs=[row_spec, row_spec,
                  pl.BlockSpec((CONV_HALO, D_MODEL), lambda i: (jnp.maximum(i * halo_blocks - 1, 0), 0)),
                  full((N_META, D_MODEL)), row_spec, row_spec, row_spec,
                  full((D_MODEL, D_MODEL)), full((D_MODEL, D_MODEL)), full((D_MODEL, D_MODEL)),
                  full((CONV_HALO, D_MODEL)), full((1, D_MODEL)), full((1, D_MODEL)), full((1, D_MODEL)),
                  full((1, D_MODEL)), full((D_MODEL, LANES)), full((1, LANES))],
        out_specs=[row_spec, pl.BlockSpec((tm * ROW_TILES, LANES), lambda i: (i, 0)),
                   pl.BlockSpec((tm, LANES), lambda i: (i, 0))],
        out_shape=[jax.ShapeDtypeStruct((rows, D_MODEL), F32),
                   jax.ShapeDtypeStruct((rows * ROW_TILES, LANES), F32),
                   jax.ShapeDtypeStruct((rows, LANES), F32)],
        scratch_shapes=[pltpu.VMEM((tm + CONV_HALO, D_MODEL), F32),
                        pltpu.VMEM((SUBLANES - 1, tm + CONV_HALO - SUBLANES, D_MODEL), F32),
                        pltpu.VMEM((tm, D_MODEL), BF16)],
        compiler_params=_cparams(("arbitrary",)),
        name="post",
    )(o, z, z, z_meta, ga, gc, x2d, wo, wpw, wout, cw, cb, lng, lnb, n2g, wr, br)


ROW_UNROLL = 4


def _tile_rows(start, n_rows=1):
    return pl.ds(pl.multiple_of(start * ROW_TILES, ROW_TILES), n_rows * ROW_TILES)


def _ffn_body(blk_e_ref, nact_ref, tok_ref, tok_next_ref, dst_ref, h2_hbm, wg_ref, wu_ref, wd_ref, y_hbm,
              xg_ref, yb_ref, gsem, ssem):
    i = pl.program_id(0)
    nact = nact_ref[0]
    slot = i % 2

    def start_gather(t_ref, sl):
        def body(j, c):
            for pr in range(2):
                r = 2 * j + pr
                pltpu.make_async_copy(h2_hbm.at[_tile_rows(t_ref[0, 0, r]), :], xg_ref.at[sl, _tile_rows(r), :],
                                      gsem.at[sl]).start(priority=pr)
            return c
        lax.fori_loop(0, MOE_BLOCK // 2, body, 0, unroll=ROW_UNROLL)

    def wait_gather(sl):
        pltpu.make_async_copy(h2_hbm.at[_tile_rows(0, MOE_BLOCK), :], xg_ref.at[sl], gsem.at[sl]).wait()

    def start_scatter():
        def body(j, c):
            for pr in range(2):
                r = 2 * j + pr
                pltpu.make_async_copy(yb_ref.at[_tile_rows(r), :], y_hbm.at[_tile_rows(dst_ref[0, 0, r]), :],
                                      ssem).start(priority=pr)
            return c
        lax.fori_loop(0, MOE_BLOCK // 2, body, 0, unroll=ROW_UNROLL)

    def wait_scatter():
        pltpu.make_async_copy(yb_ref, y_hbm.at[_tile_rows(0, MOE_BLOCK), :], ssem).wait()

    @pl.when(i == 0)
    def _():
        yb_ref[...] = jnp.zeros_like(yb_ref)
        n_rows = y_hbm.shape[0] // ROW_TILES
        spare = pltpu.make_async_copy(yb_ref, y_hbm.at[_tile_rows(n_rows - MOE_BLOCK, MOE_BLOCK), :], ssem)
        spare.start()
        spare.wait()

    @pl.when((i == 0) & (nact > 0))
    def _():
        start_gather(tok_ref, 0)

    @pl.when(i < nact)
    def _():
        wait_gather(slot)

        @pl.when(i + 1 < nact)
        def _():
            start_gather(tok_next_ref, 1 - slot)

        xb = jnp.concatenate([xg_ref[slot, pl.ds(s, MOE_BLOCK, stride=ROW_TILES), :] for s in range(ROW_TILES)],
                             axis=1).astype(BF16)
        g = jnp.dot(xb, wg_ref[0], preferred_element_type=F32)
        u = jnp.dot(xb, wu_ref[0], preferred_element_type=F32)
        hmid = ((g * jax.nn.sigmoid(g)) * u).astype(BF16)
        y = jnp.dot(hmid, wd_ref[0], preferred_element_type=F32)

        @pl.when(i > 0)
        def _():
            wait_scatter()

        for s in range(ROW_TILES):
            yb_ref[pl.ds(s, MOE_BLOCK, stride=ROW_TILES), :] = y[:, s * LANES:(s + 1) * LANES]
        start_scatter()

        @pl.when(i == nact - 1)
        def _():
            wait_scatter()


def _ffn(blk_e, nact, tok, dst, h2_tiles, wg, wu, wd, n_out_rows):
    nb = blk_e.shape[0]
    smem_block = (1, 1, MOE_BLOCK)
    cur = pl.BlockSpec(smem_block, lambda i, be, na: (i, 0, 0), memory_space=pltpu.SMEM)
    nxt = pl.BlockSpec(smem_block, lambda i, be, na: (jnp.minimum(i + 1, nb - 1), 0, 0), memory_space=pltpu.SMEM)
    grid_spec = pltpu.PrefetchScalarGridSpec(
        num_scalar_prefetch=2,
        grid=(nb,),
        in_specs=[cur, nxt, cur,
                  pl.BlockSpec(memory_space=pl.ANY),
                  pl.BlockSpec((1, D_MODEL, D_EXPERT), lambda i, be, na: (be[i], 0, 0)),
                  pl.BlockSpec((1, D_MODEL, D_EXPERT), lambda i, be, na: (be[i], 0, 0)),
                  pl.BlockSpec((1, D_EXPERT, D_MODEL), lambda i, be, na: (be[i], 0, 0))],
        out_specs=pl.BlockSpec(memory_space=pl.ANY),
        scratch_shapes=[pltpu.VMEM((2, MOE_BLOCK * ROW_TILES, LANES), F32),
                        pltpu.VMEM((MOE_BLOCK * ROW_TILES, LANES), F32),
                        pltpu.SemaphoreType.DMA((2,)), pltpu.SemaphoreType.DMA(())],
    )
    return pl.pallas_call(
        _ffn_body,
        grid_spec=grid_spec,
        out_shape=jax.ShapeDtypeStruct((n_out_rows * ROW_TILES, LANES), F32),
        compiler_params=_cparams(("arbitrary",)),
        name="ffn",
    )(blk_e, nact, tok, tok, dst, h2_tiles, wg, wu, wd)


def _dispatch(route, n_tok):
    eid = route[:, 2:2 + TOP_K].astype(jnp.int32)
    n_assign = n_tok * TOP_K
    flat_e = eid.reshape(n_assign)
    order = jnp.argsort(flat_e, stable=True).astype(jnp.int32)
    experts = jnp.arange(N_EXPERTS, dtype=jnp.int32)
    counts = jnp.sum((flat_e[None, :] == experts[:, None]).astype(jnp.int32), axis=1)
    starts = jnp.cumsum(counts) - counts
    pcounts = (counts + MOE_BLOCK - 1) // MOE_BLOCK * MOE_BLOCK
    pends = jnp.cumsum(pcounts)
    pstarts = pends - pcounts
    nb = -(-n_assign // MOE_BLOCK) + N_EXPERTS
    blk_start = jnp.arange(nb, dtype=jnp.int32) * MOE_BLOCK
    blk_e = jnp.minimum(jnp.sum((pends[None, :] <= blk_start[:, None]).astype(jnp.int32), axis=1), N_EXPERTS - 1)
    blk_cnt = jnp.clip(counts[blk_e] - (blk_start - pstarts[blk_e]), 0, MOE_BLOCK).astype(jnp.int32)
    within = jnp.arange(MOE_BLOCK, dtype=jnp.int32)[None, :]
    valid = within < blk_cnt[:, None]
    src = jnp.clip((starts[blk_e] + blk_start - pstarts[blk_e])[:, None] + within, 0, n_assign - 1)
    assign = order[src]
    tok = jnp.where(valid, assign // TOP_K, 0).astype(jnp.int32)
    dst = jnp.where(valid, (assign % TOP_K) * n_tok + assign // TOP_K, TOP_K * n_tok + within).astype(jnp.int32)
    n_active = (pends[-1:] // MOE_BLOCK).astype(jnp.int32)
    return blk_e, n_active, tok.reshape(nb, 1, MOE_BLOCK), dst.reshape(nb, 1, MOE_BLOCK)


def _final_body(hres_ref, y0_ref, y1_ref, route_ref, g_ref, out_ref, *, tm):
    y0 = jnp.concatenate([y0_ref[pl.ds(s, tm, stride=ROW_TILES), :] for s in range(ROW_TILES)], axis=1)
    y1 = jnp.concatenate([y1_ref[pl.ds(s, tm, stride=ROW_TILES), :] for s in range(ROW_TILES)], axis=1)
    route = route_ref[...]
    h = hres_ref[...] + (y0 * route[:, 0:1] + y1 * route[:, 1:2])
    ms = jnp.mean(h * h, axis=-1, keepdims=True)
    out_ref[...] = (h * lax.rsqrt(ms + EPS)) * g_ref[...]


def _final(h_res, y_tiles, route, final_g, seq):
    rows = h_res.shape[0]
    tm = min(256, seq)
    nt = rows // tm
    row_spec = pl.BlockSpec((tm, D_MODEL), lambda i: (i, 0))
    return pl.pallas_call(
        functools.partial(_final_body, tm=tm),
        grid=(nt,),
        in_specs=[row_spec,
                  pl.BlockSpec((tm * ROW_TILES, LANES), lambda i: (i, 0)),
                  pl.BlockSpec((tm * ROW_TILES, LANES), lambda i: (i + nt, 0)),
                  pl.BlockSpec((tm, LANES), lambda i: (i, 0)),
                  pl.BlockSpec((1, D_MODEL), lambda i: (0, 0))],
        out_specs=row_spec,
        out_shape=jax.ShapeDtypeStruct((rows, D_MODEL), F32),
        compiler_params=_cparams(("arbitrary",)),
        name="final",
    )(h_res, y_tiles, y_tiles, route, final_g)


def kernel(x, meta_tokens, norm1_g, w_in, lam_q1, lam_k1, lam_q2, lam_k2, subln_g, w_o_attn, conv_w, conv_b,
           conv_ln_g, conv_ln_b, w_pw2, w_out, norm2_g, w_group, b_group, w_router, b_router, w_e_gate,
           w_e_up, w_e_down, final_g):
    batch, seq, _ = x.shape
    n_tok = batch * seq
    layer = 0
    lam_init = 0.8 - 0.6 * math.exp(-0.3 * layer)
    row = lambda a: a.reshape(1, -1).astype(F32)

    x2d = x.reshape(n_tok, D_MODEL)
    w_in_b = w_in[layer].astype(BF16)
    rc, rlo, rhi = _rope_tables(N_META + seq)
    tm_in = min(512, seq)
    q, k, v, z, ga, gc = _inproj(x2d, row(norm1_g[layer]), w_in_b, rc[N_META:], rlo[N_META:], rhi[N_META:], tm_in)
    _, k_meta, v_meta, z_meta, _, _ = _inproj(meta_tokens.astype(F32), row(norm1_g[layer]), w_in_b,
                                              rc[:N_META], rlo[:N_META], rhi[:N_META], N_META)

    o = _attention(q, k, v, k_meta, v_meta, row(lam_q1[layer]), row(lam_k1[layer]), row(lam_q2[layer]),
                   row(lam_k2[layer]), row(subln_g[layer]), batch, seq, lam_init)

    cw = jnp.concatenate([conv_w[layer].astype(F32), jnp.zeros((CONV_HALO - CONV_K, D_MODEL), F32)], axis=0)
    pad_cols = LANES - N_EXPERTS - N_GROUPS
    wr = jnp.concatenate([w_router[layer].astype(F32), w_group[layer].astype(F32),
                          jnp.zeros((D_MODEL, pad_cols), F32)], axis=1)
    br = jnp.concatenate([b_router[layer].astype(F32), b_group[layer].astype(F32),
                          jnp.zeros((pad_cols,), F32)]).reshape(1, LANES)
    h_res, h2_tiles, route = _post(o, z, z_meta, ga, gc, x2d, w_o_attn[layer].astype(BF16),
                                   w_pw2[layer].astype(BF16), w_out[layer].astype(BF16), cw,
                                   row(conv_b[layer]), row(conv_ln_g[layer]), row(conv_ln_b[layer]),
                                   row(norm2_g[layer]), wr, br, seq)

    blk_e, n_active, tok, dst = _dispatch(route, n_tok)
    y_tiles = _ffn(blk_e, n_active, tok, dst, h2_tiles, w_e_gate[layer].astype(BF16), w_e_up[layer].astype(BF16),
                   w_e_down[layer].astype(BF16), TOP_K * n_tok + MOE_BLOCK)

    out = _final(h_res, y_tiles, route, row(final_g), seq)
    return out.reshape(batch, seq, D_MODEL)
```

```python
import functools
import math

import jax
import jax.numpy as jnp
import numpy as np
from jax import lax
from jax.experimental import pallas as pl
from jax.experimental.pallas import tpu as pltpu

F32 = jnp.float32
BF16 = jnp.bfloat16

D_MODEL = 1024
CHUNK = 64
N_META = 16
N_HEADS = 8
HEAD_DIM = 64
ROT_DIM = HEAD_DIM // 4
ROPE_THETA = 500000.0
ATTN_W = N_HEADS * 2 * HEAD_DIM
CONV_K = 31
N_GROUPS = 4
EXPERTS_PER_GROUP = 8
N_EXPERTS = N_GROUPS * EXPERTS_PER_GROUP
TOP_K = 2
D_EXPERT = D_MODEL // 2
MOE_BLOCK = 512
EPS = 1e-6
LANES = 128
SUBLANES = 8
ROW_TILES = D_MODEL // LANES
N_PROJ = 7
CONV_HALO = 32
VMEM_LIMIT = 48 * 1024 * 1024


def _cparams(sem):
    return pltpu.CompilerParams(dimension_semantics=sem, vmem_limit_bytes=VMEM_LIMIT)


def _rope(t, c, s_lo, s_hi):
    half = ROT_DIM // 2
    n = t.shape[-1]
    return t * c + pltpu.roll(t, n - half, 1) * s_lo + pltpu.roll(t, half, 1) * s_hi


def _inproj_body(x_ref, g_ref, w_ref, rc_ref, rlo_ref, rhi_ref,
                 q_ref, k_ref, v_ref, z_ref, ga_ref, gc_ref):
    x = x_ref[...]
    ms = jnp.mean(x * x, axis=-1, keepdims=True)
    xn = ((x * lax.rsqrt(ms + EPS)) * g_ref[...]).astype(BF16)

    def proj(j):
        return jnp.dot(xn, w_ref[:, j * D_MODEL:(j + 1) * D_MODEL], preferred_element_type=F32)

    reps = (1, D_MODEL // LANES)
    c, lo, hi = jnp.tile(rc_ref[...], reps), jnp.tile(rlo_ref[...], reps), jnp.tile(rhi_ref[...], reps)
    q_ref[...] = (_rope(proj(0), c, lo, hi) * (HEAD_DIM ** -0.5 * math.log2(math.e))).astype(BF16)
    k_ref[...] = _rope(proj(1), c, lo, hi).astype(BF16)
    v_ref[...] = proj(2).astype(BF16)
    z_ref[...] = (proj(3) * jax.nn.sigmoid(proj(4))).astype(BF16)
    ga_ref[...] = jax.nn.sigmoid(proj(5)).astype(BF16)
    gc_ref[...] = jax.nn.sigmoid(proj(6)).astype(BF16)


def _inproj(x2d, g, w_bf16, rc, rlo, rhi, tm):
    rows = x2d.shape[0]
    tab_blocks = rc.shape[0] // tm
    row_spec = pl.BlockSpec((tm, D_MODEL), lambda i: (i, 0))
    tab_spec = pl.BlockSpec((tm, LANES), lambda i: (i % tab_blocks, 0))
    out = jax.ShapeDtypeStruct((rows, D_MODEL), BF16)
    return pl.pallas_call(
        _inproj_body,
        grid=(rows // tm,),
        in_specs=[row_spec,
                  pl.BlockSpec((1, D_MODEL), lambda i: (0, 0)),
                  pl.BlockSpec((D_MODEL, N_PROJ * D_MODEL), lambda i: (0, 0), pipeline_mode=pl.Buffered(1)),
                  tab_spec, tab_spec, tab_spec],
        out_specs=[row_spec] * 6,
        out_shape=[out] * 6,
        compiler_params=_cparams(("arbitrary",)),
        name="inproj",
    )(x2d, g, w_bf16, rc, rlo, rhi)


def _rope_tables(length):
    half = ROT_DIM // 2
    inv_freq = np.float32(ROPE_THETA) ** (-np.arange(0, ROT_DIM, 2, dtype=np.float32) / np.float32(ROT_DIM))
    ang = (np.arange(length, dtype=np.float32)[:, None] * inv_freq[None, :]).astype(np.float32)
    cos, sin = np.cos(ang).astype(np.float32), np.sin(ang).astype(np.float32)
    ones = np.ones((length, HEAD_DIM - ROT_DIM), np.float32)
    c = np.concatenate([cos, cos, ones], axis=1)
    lo = np.concatenate([-sin, np.zeros((length, HEAD_DIM - half), np.float32)], axis=1)
    hi = np.concatenate([np.zeros((length, half), np.float32), sin,
                         np.zeros((length, HEAD_DIM - ROT_DIM), np.float32)], axis=1)
    rep = LANES // HEAD_DIM
    return tuple(jnp.asarray(np.tile(t, (1, rep))) for t in (c, lo, hi))


def _dot_nt(a, b):
    return lax.dot_general(a, b, (((1,), (1,)), ((), ())), preferred_element_type=F32)


def _attn_body(lq1_ref, lk1_ref, lq2_ref, lk2_ref, sg_ref, q_ref, k_ref, v_ref, km_ref, vm_ref,
               o_ref, *, seq, tq, tk_max, lam_init):
    lam = (jnp.exp(jnp.sum(lq1_ref[...] * lk1_ref[...], keepdims=True))
           - jnp.exp(jnp.sum(lq2_ref[...] * lk2_ref[...], keepdims=True)) + lam_init)
    lane = lax.broadcasted_iota(jnp.int32, (tq, LANES), 1)
    lane2 = lax.broadcasted_iota(jnp.int32, (2 * tq, LANES), 1)
    neg = -jnp.inf
    km = km_ref[...]
    vm = vm_ref[...]

    for qi in range(seq // tq):
        q = q_ref[qi * tq:(qi + 1) * tq, :]
        zero = jnp.zeros_like(q)
        q2 = jnp.concatenate([jnp.where(lane < HEAD_DIM, q, zero), jnp.where(lane >= HEAD_DIM, q, zero)], axis=0)
        n_keys = (qi + 1) * tq
        blocks = [(start, min(tk_max, n_keys - start)) for start in range(0, n_keys, tk_max)]

        def scores(start, width):
            s = _dot_nt(q2, k_ref[start:start + width, :])
            if start + width > qi * tq:
                row_chunk = (qi * tq + lax.broadcasted_iota(jnp.int32, s.shape, 0) % tq) // CHUNK
                col_chunk = (start + lax.broadcasted_iota(jnp.int32, s.shape, 1)) // CHUNK
                s = jnp.where(col_chunk <= row_chunk, s, neg)
            tiles = [s[:, t * LANES:(t + 1) * LANES] for t in range(width // LANES)]
            if start == 0:
                tiles.append(jnp.where(lane2 < N_META, _dot_nt(q2, km), neg))
            return tiles

        m = l = acc = None
        tiles_next = scores(*blocks[0])
        for bi, (start, width) in enumerate(blocks):
            tiles = tiles_next
            if bi + 1 < len(blocks):
                tiles_next = scores(*blocks[bi + 1])
            tile_max = functools.reduce(jnp.maximum, tiles)
            m_blk = jnp.broadcast_to(jnp.max(tile_max, axis=-1, keepdims=True), tile_max.shape)
            m_new = m_blk if m is None else jnp.maximum(m, m_blk)
            p_tiles = [jnp.exp2(t - m_new) for t in tiles]
            n_real = width // LANES
            p_sum = functools.reduce(jnp.add, p_tiles)
            pv = jnp.dot(jnp.concatenate(p_tiles[:n_real], axis=1).astype(BF16), v_ref[start:start + width, :],
                         preferred_element_type=F32)
            if start == 0:
                pv = pv + jnp.dot(p_tiles[n_real].astype(BF16), vm, preferred_element_type=F32)
            if m is None:
                l, acc = p_sum, pv
            else:
                alpha = jnp.exp2(m - m_new)
                l = alpha * l + p_sum
                acc = alpha * acc + pv
            m = m_new

        o_maps = acc / jnp.sum(l, axis=-1, keepdims=True)
        o = o_maps[:tq] - lam * o_maps[tq:]
        ms = jnp.mean(o * o, axis=-1, keepdims=True)
        o = ((o * lax.rsqrt(ms + EPS)) * sg_ref[...]) * (1.0 - lam_init)
        o_ref[qi * tq:(qi + 1) * tq, :] = o.astype(BF16)


def _attention(q, k, v, k_meta, v_meta, lq1, lk1, lq2, lk2, subln_g, batch, seq, lam_init):
    tq = min(256, seq)
    head_spec = pl.BlockSpec((seq, LANES), lambda b, h: (b, h))
    meta_spec = pl.BlockSpec((LANES, LANES), lambda b, h: (0, h))
    lam_spec = pl.BlockSpec((1, HEAD_DIM), lambda b, h: (0, 0))
    pad = jnp.zeros((LANES - N_META, k_meta.shape[1]), k_meta.dtype)
    k_meta = jnp.concatenate([k_meta, pad], axis=0)
    v_meta = jnp.concatenate([v_meta, pad], axis=0)
    body = functools.partial(_attn_body, seq=seq, tq=tq, tk_max=512, lam_init=lam_init)
    return pl.pallas_call(
        body,
        grid=(batch, N_HEADS),
        in_specs=[lam_spec] * 4 + [pl.BlockSpec((1, LANES), lambda b, h: (0, 0)),
                                   head_spec, head_spec, head_spec, meta_spec, meta_spec],
        out_specs=head_spec,
        out_shape=jax.ShapeDtypeStruct(q.shape, BF16),
        compiler_params=_cparams(("arbitrary", "arbitrary")),
        name="attn",
    )(lq1, lk1, lq2, lk2, subln_g, q, k, v, k_meta, v_meta)


def _post_body(o_ref, z_ref, zprev_ref, zmeta_ref, ga_ref, gc_ref, x_ref,
               wo_ref, wpw_ref, wout_ref, cw_ref, cb_ref, lng_ref, lnb_ref, n2g_ref, wr_ref, br_ref,
               hres_ref, h2_ref, route_ref, zext_ref, zph_ref, conv_ref, zc_ref, *, tm, tiles_per_seq, conv_rows):
    i = pl.program_id(0)
    first = (i % tiles_per_seq) == 0
    meta_halo = jnp.concatenate(
        [jnp.zeros((CONV_HALO - N_META, D_MODEL), F32), zmeta_ref[...].astype(F32)], axis=0)
    zext_ref[0:CONV_HALO, :] = jnp.where(first, meta_halo, zprev_ref[...].astype(F32))
    zext_ref[CONV_HALO:, :] = z_ref[...].astype(F32)
    ph_rows = zph_ref.shape[1]
    for b in range(1, SUBLANES):
        zph_ref[b - 1] = zext_ref[b:b + ph_rows, :]

    base = CONV_HALO - (CONV_K - 1)
    reps = conv_rows // SUBLANES
    for j in range(D_MODEL // LANES):
        cols = slice(j * LANES, (j + 1) * LANES)
        for c in range(tm // conv_rows):
            r0 = c * conv_rows
            acc = jnp.zeros((conv_rows, LANES), F32)
            for kk in range(CONV_K):
                a, b = divmod(base + kk, SUBLANES)
                lo = r0 + a * SUBLANES
                win = zext_ref[lo:lo + conv_rows, cols] if b == 0 else zph_ref[b - 1, lo:lo + conv_rows, cols]
                acc = acc + win * jnp.tile(cw_ref[kk, :, cols], (reps, 1))
            conv_ref[r0:r0 + conv_rows, cols] = acc

    cb = cb_ref[...]
    lng = lng_ref[...]
    lnb = lnb_ref[...]
    ln_rows = 32
    for c in range(tm // ln_rows):
        r0 = c * ln_rows
        zc = conv_ref[r0:r0 + ln_rows, :] + cb
        mu = jnp.mean(zc, axis=-1, keepdims=True)
        var = jnp.mean(jnp.square(zc - mu), axis=-1, keepdims=True)
        y = ((zc - mu) * lax.rsqrt(var + EPS)) * lng + lnb
        zc_ref[r0:r0 + ln_rows, :] = (y * jax.nn.sigmoid(y)).astype(BF16)

    y_attn = jnp.dot(o_ref[...], wo_ref[...], preferred_element_type=F32)
    y_conv = jnp.dot(zc_ref[...], wpw_ref[...], preferred_element_type=F32)
    mix = ga_ref[...] * y_attn.astype(BF16) + gc_ref[...] * y_conv.astype(BF16)
    h_res = x_ref[...] + jnp.dot(mix, wout_ref[...], preferred_element_type=F32)
    hres_ref[...] = h_res

    ms = jnp.mean(h_res * h_res, axis=-1, keepdims=True)
    h2 = (h_res * lax.rsqrt(ms + EPS)) * n2g_ref[...]
    for s in range(ROW_TILES):
        h2_ref[pl.ds(s, tm, stride=ROW_TILES), :] = h2[:, s * LANES:(s + 1) * LANES]

    logits = jnp.dot(h2.astype(BF16), wr_ref[...], preferred_element_type=F32) + br_ref[...]
    lane = lax.broadcasted_iota(jnp.int32, (tm, LANES), 1)
    neg = -jnp.inf
    big = jnp.int32(1 << 20)
    gl = jnp.where((lane >= N_EXPERTS) & (lane < N_EXPERTS + N_GROUPS), logits, neg)
    gmax = jnp.max(gl, axis=-1, keepdims=True)
    gidx = jnp.min(jnp.where(gl == gmax, lane - N_EXPERTS, big), axis=-1, keepdims=True)
    g_w = 1.0 / jnp.sum(jnp.exp(gl - gmax), axis=-1, keepdims=True)
    el = jnp.where((lane < N_EXPERTS) & ((lane // EXPERTS_PER_GROUP) == gidx), logits, neg)
    e1 = jnp.max(el, axis=-1, keepdims=True)
    i1 = jnp.min(jnp.where(el == e1, lane, big), axis=-1, keepdims=True)
    el2 = jnp.where(lane == i1, neg, el)
    e2 = jnp.max(el2, axis=-1, keepdims=True)
    i2 = jnp.min(jnp.where(el2 == e2, lane, big), axis=-1, keepdims=True)
    esum = jnp.sum(jnp.exp(el - e1), axis=-1, keepdims=True)
    p1 = 1.0 / esum
    p2 = jnp.exp(e2 - e1) / esum
    w1 = p1 / (p1 + p2)
    w2 = p2 / (p1 + p2)
    route = jnp.where(lane == 0, g_w * w1, 0.0)
    route = jnp.where(lane == 1, g_w * w2, route)
    route = jnp.where(lane == 2, i1.astype(F32), route)
    route = jnp.where(lane == 3, i2.astype(F32), route)
    route_ref[...] = route


def _post(o, z, z_meta, ga, gc, x2d, wo, wpw, wout, cw, cb, lng, lnb, n2g, wr, br, seq):
    rows = x2d.shape[0]
    tm = min(256, seq)
    tiles_per_seq = seq // tm
    conv_rows = min(64, tm)
    row_spec = pl.BlockSpec((tm, D_MODEL), lambda i: (i, 0))
    full = lambda shape: pl.BlockSpec(shape, lambda i: (0, 0))
    halo_blocks = tm // CONV_HALO
    cw = jnp.broadcast_to(cw[:, None, :], (CONV_K, SUBLANES, D_MODEL))
    body = functools.partial(_post_body, tm=tm, tiles_per_seq=tiles_per_seq, conv_rows=conv_rows)
    return pl.pallas_call(
        body,
        grid=(rows // tm,),
        in_specs=[row_spec, row_spec,
                  pl.BlockSpec((CONV_HALO, D_MODEL), lambda i: (jnp.maximum(i * halo_blocks - 1, 0), 0)),
                  full((N_META, D_MODEL)), row_spec, row_spec, row_spec,
                  full((D_MODEL, D_MODEL)), full((D_MODEL, D_MODEL)), full((D_MODEL, D_MODEL)),
                  pl.BlockSpec((CONV_K, SUBLANES, D_MODEL), lambda i: (0, 0, 0)),
                  full((1, D_MODEL)), full((1, D_MODEL)), full((1, D_MODEL)),
                  full((1, D_MODEL)), full((D_MODEL, LANES)), full((1, LANES))],
        out_specs=[row_spec, pl.BlockSpec((tm * ROW_TILES, LANES), lambda i: (i, 0)),
                   pl.BlockSpec((tm, LANES), lambda i: (i, 0))],
        out_shape=[jax.ShapeDtypeStruct((rows, D_MODEL), F32),
                   jax.ShapeDtypeStruct((rows * ROW_TILES, LANES), F32),
                   jax.ShapeDtypeStruct((rows, LANES), F32)],
        scratch_shapes=[pltpu.VMEM((tm + CONV_HALO, D_MODEL), F32),
                        pltpu.VMEM((SUBLANES - 1, tm + CONV_HALO - SUBLANES, D_MODEL), F32),
                        pltpu.VMEM((tm, D_MODEL), F32),
                        pltpu.VMEM((tm, D_MODEL), BF16)],
        compiler_params=_cparams(("arbitrary",)),
        name="post",
    )(o, z, z, z_meta, ga, gc, x2d, wo, wpw, wout, cw, cb, lng, lnb, n2g, wr, br)


ROW_UNROLL = 4


def _tile_rows(start, n_rows=1):
    return pl.ds(pl.multiple_of(start * ROW_TILES, ROW_TILES), n_rows * ROW_TILES)


def _ffn_body(blk_e_ref, nact_ref, tok_ref, tok_next_ref, dst_ref, h2_hbm, wg_ref, wu_ref, wd_ref, y_hbm,
              xg_ref, yb_ref, gsem, ssem):
    i = pl.program_id(0)
    nact = nact_ref[0]
    slot = i % 2

    def start_gather(t_ref, sl):
        def body(j, c):
            for pr in range(2):
                r = 2 * j + pr
                pltpu.make_async_copy(h2_hbm.at[_tile_rows(t_ref[0, 0, r]), :], xg_ref.at[sl, _tile_rows(r), :],
                                      gsem.at[sl]).start(priority=pr)
            return c
        lax.fori_loop(0, MOE_BLOCK // 2, body, 0, unroll=ROW_UNROLL)

    def wait_gather(sl):
        pltpu.make_async_copy(h2_hbm.at[_tile_rows(0, MOE_BLOCK), :], xg_ref.at[sl], gsem.at[sl]).wait()

    def start_scatter():
        def body(j, c):
            for pr in range(2):
                r = 2 * j + pr
                pltpu.make_async_copy(yb_ref.at[_tile_rows(r), :], y_hbm.at[_tile_rows(dst_ref[0, 0, r]), :],
                                      ssem).start(priority=pr)
            return c
        lax.fori_loop(0, MOE_BLOCK // 2, body, 0, unroll=ROW_UNROLL)

    def wait_scatter():
        pltpu.make_async_copy(yb_ref, y_hbm.at[_tile_rows(0, MOE_BLOCK), :], ssem).wait()

    @pl.when(i == 0)
    def _():
        yb_ref[...] = jnp.zeros_like(yb_ref)
        n_rows = y_hbm.shape[0] // ROW_TILES
        spare = pltpu.make_async_copy(yb_ref, y_hbm.at[_tile_rows(n_rows - MOE_BLOCK, MOE_BLOCK), :], ssem)
        spare.start()
        spare.wait()

    @pl.when((i == 0) & (nact > 0))
    def _():
        start_gather(tok_ref, 0)

    @pl.when(i < nact)
    def _():
        wait_gather(slot)

        @pl.when(i + 1 < nact)
        def _():
            start_gather(tok_next_ref, 1 - slot)

        xb = jnp.concatenate([xg_ref[slot, pl.ds(s, MOE_BLOCK, stride=ROW_TILES), :] for s in range(ROW_TILES)],
                             axis=1).astype(BF16)
        g = jnp.dot(xb, wg_ref[0], preferred_element_type=F32)
        u = jnp.dot(xb, wu_ref[0], preferred_element_type=F32)
        hmid = ((g * jax.nn.sigmoid(g)) * u).astype(BF16)
        y = jnp.dot(hmid, wd_ref[0], preferred_element_type=F32)

        @pl.when(i > 0)
        def _():
            wait_scatter()

        for s in range(ROW_TILES):
            yb_ref[pl.ds(s, MOE_BLOCK, stride=ROW_TILES), :] = y[:, s * LANES:(s + 1) * LANES]
        start_scatter()

        @pl.when(i == nact - 1)
        def _():
            wait_scatter()


def _ffn(blk_e, nact, tok, dst, h2_tiles, wg, wu, wd, n_out_rows):
    nb = blk_e.shape[0]
    smem_block = (1, 1, MOE_BLOCK)
    cur = pl.BlockSpec(smem_block, lambda i, be, na: (i, 0, 0), memory_space=pltpu.SMEM)
    nxt = pl.BlockSpec(smem_block, lambda i, be, na: (jnp.minimum(i + 1, nb - 1), 0, 0), memory_space=pltpu.SMEM)
    grid_spec = pltpu.PrefetchScalarGridSpec(
        num_scalar_prefetch=2,
        grid=(nb,),
        in_specs=[cur, nxt, cur,
                  pl.BlockSpec(memory_space=pl.ANY),
                  pl.BlockSpec((1, D_MODEL, D_EXPERT), lambda i, be, na: (be[i], 0, 0)),
                  pl.BlockSpec((1, D_MODEL, D_EXPERT), lambda i, be, na: (be[i], 0, 0)),
                  pl.BlockSpec((1, D_EXPERT, D_MODEL), lambda i, be, na: (be[i], 0, 0))],
        out_specs=pl.BlockSpec(memory_space=pl.ANY),
        scratch_shapes=[pltpu.VMEM((2, MOE_BLOCK * ROW_TILES, LANES), F32),
                        pltpu.VMEM((MOE_BLOCK * ROW_TILES, LANES), F32),
                        pltpu.SemaphoreType.DMA((2,)), pltpu.SemaphoreType.DMA(())],
    )
    return pl.pallas_call(
        _ffn_body,
        grid_spec=grid_spec,
        out_shape=jax.ShapeDtypeStruct((n_out_rows * ROW_TILES, LANES), F32),
        compiler_params=_cparams(("arbitrary",)),
        name="ffn",
    )(blk_e, nact, tok, tok, dst, h2_tiles, wg, wu, wd)


def _dispatch(route, n_tok):
    eid = route[:, 2:2 + TOP_K].astype(jnp.int32)
    n_assign = n_tok * TOP_K
    flat_e = eid.reshape(n_assign)
    order = jnp.argsort(flat_e, stable=True).astype(jnp.int32)
    experts = jnp.arange(N_EXPERTS, dtype=jnp.int32)
    counts = jnp.sum((flat_e[None, :] == experts[:, None]).astype(jnp.int32), axis=1)
    starts = jnp.cumsum(counts) - counts
    pcounts = (counts + MOE_BLOCK - 1) // MOE_BLOCK * MOE_BLOCK
    pends = jnp.cumsum(pcounts)
    pstarts = pends - pcounts
    nb = -(-n_assign // MOE_BLOCK) + N_EXPERTS
    blk_start = jnp.arange(nb, dtype=jnp.int32) * MOE_BLOCK
    blk_e = jnp.minimum(jnp.sum((pends[None, :] <= blk_start[:, None]).astype(jnp.int32), axis=1), N_EXPERTS - 1)
    blk_cnt = jnp.clip(counts[blk_e] - (blk_start - pstarts[blk_e]), 0, MOE_BLOCK).astype(jnp.int32)
    within = jnp.arange(MOE_BLOCK, dtype=jnp.int32)[None, :]
    valid = within < blk_cnt[:, None]
    src = jnp.clip((starts[blk_e] + blk_start - pstarts[blk_e])[:, None] + within, 0, n_assign - 1)
    assign = order[src]
    tok = jnp.where(valid, assign // TOP_K, 0).astype(jnp.int32)
    dst = jnp.where(valid, (assign % TOP_K) * n_tok + assign // TOP_K, TOP_K * n_tok + within).astype(jnp.int32)
    n_active = (pends[-1:] // MOE_BLOCK).astype(jnp.int32)
    return blk_e, n_active, tok.reshape(nb, 1, MOE_BLOCK), dst.reshape(nb, 1, MOE_BLOCK)


def _final_body(hres_ref, y0_ref, y1_ref, route_ref, g_ref, out_ref, *, tm):
    y0 = jnp.concatenate([y0_ref[pl.ds(s, tm, stride=ROW_TILES), :] for s in range(ROW_TILES)], axis=1)
    y1 = jnp.concatenate([y1_ref[pl.ds(s, tm, stride=ROW_TILES), :] for s in range(ROW_TILES)], axis=1)
    route = route_ref[...]
    h = hres_ref[...] + (y0 * route[:, 0:1] + y1 * route[:, 1:2])
    ms = jnp.mean(h * h, axis=-1, keepdims=True)
    out_ref[...] = (h * lax.rsqrt(ms + EPS)) * g_ref[...]


def _final(h_res, y_tiles, route, final_g, seq):
    rows = h_res.shape[0]
    tm = min(256, seq)
    nt = rows // tm
    row_spec = pl.BlockSpec((tm, D_MODEL), lambda i: (i, 0))
    return pl.pallas_call(
        functools.partial(_final_body, tm=tm),
        grid=(nt,),
        in_specs=[row_spec,
                  pl.BlockSpec((tm * ROW_TILES, LANES), lambda i: (i, 0)),
                  pl.BlockSpec((tm * ROW_TILES, LANES), lambda i: (i + nt, 0)),
                  pl.BlockSpec((tm, LANES), lambda i: (i, 0)),
                  pl.BlockSpec((1, D_MODEL), lambda i: (0, 0))],
        out_specs=row_spec,
        out_shape=jax.ShapeDtypeStruct((rows, D_MODEL), F32),
        compiler_params=_cparams(("arbitrary",)),
        name="final",
    )(h_res, y_tiles, y_tiles, route, final_g)


def kernel(x, meta_tokens, norm1_g, w_in, lam_q1, lam_k1, lam_q2, lam_k2, subln_g, w_o_attn, conv_w, conv_b,
           conv_ln_g, conv_ln_b, w_pw2, w_out, norm2_g, w_group, b_group, w_router, b_router, w_e_gate,
           w_e_up, w_e_down, final_g):
    batch, seq, _ = x.shape
    n_tok = batch * seq
    layer = 0
    lam_init = 0.8 - 0.6 * math.exp(-0.3 * layer)
    row = lambda a: a.reshape(1, -1).astype(F32)

    x2d = x.reshape(n_tok, D_MODEL)
    w_in_b = w_in[layer].astype(BF16)
    rc, rlo, rhi = _rope_tables(N_META + seq)
    tm_in = min(512, seq)
    q, k, v, z, ga, gc = _inproj(x2d, row(norm1_g[layer]), w_in_b, rc[N_META:], rlo[N_META:], rhi[N_META:], tm_in)
    _, k_meta, v_meta, z_meta, _, _ = _inproj(meta_tokens.astype(F32), row(norm1_g[layer]), w_in_b,
                                              rc[:N_META], rlo[:N_META], rhi[:N_META], N_META)

    o = _attention(q, k, v, k_meta, v_meta, row(lam_q1[layer]), row(lam_k1[layer]), row(lam_q2[layer]),
                   row(lam_k2[layer]), row(subln_g[layer]), batch, seq, lam_init)

    cw = conv_w[layer].astype(F32)
    pad_cols = LANES - N_EXPERTS - N_GROUPS
    wr = jnp.concatenate([w_router[layer].astype(F32), w_group[layer].astype(F32),
                          jnp.zeros((D_MODEL, pad_cols), F32)], axis=1).astype(BF16)
    br = jnp.concatenate([b_router[layer].astype(F32), b_group[layer].astype(F32),
                          jnp.zeros((pad_cols,), F32)]).reshape(1, LANES)
    h_res, h2_tiles, route = _post(o, z, z_meta, ga, gc, x2d, w_o_attn[layer].astype(BF16),
                                   w_pw2[layer].astype(BF16), w_out[layer].astype(BF16), cw,
                                   row(conv_b[layer]), row(conv_ln_g[layer]), row(conv_ln_b[layer]),
                                   row(norm2_g[layer]), wr, br, seq)

    blk_e, n_active, tok, dst = _dispatch(route, n_tok)
    y_tiles = _ffn(blk_e, n_active, tok, dst, h2_tiles, w_e_gate[layer].astype(BF16), w_e_up[layer].astype(BF16),
                   w_e_down[layer].astype(BF16), TOP_K * n_tok + MOE_BLOCK)

    out = _final(h_res, y_tiles, route, row(final_g), seq)
    return out.reshape(batch, seq, D_MODEL)
```

```python
import functools
import math

import jax
import jax.numpy as jnp
import numpy as np
from jax import lax
from jax.experimental import pallas as pl
from jax.experimental.pallas import tpu as pltpu

F32 = jnp.float32
BF16 = jnp.bfloat16

D_MODEL = 1024
CHUNK = 64
N_META = 16
N_HEADS = 8
HEAD_DIM = 64
ROT_DIM = HEAD_DIM // 4
ROPE_THETA = 500000.0
ATTN_W = N_HEADS * 2 * HEAD_DIM
CONV_K = 31
N_GROUPS = 4
EXPERTS_PER_GROUP = 8
N_EXPERTS = N_GROUPS * EXPERTS_PER_GROUP
TOP_K = 2
D_EXPERT = D_MODEL // 2
MOE_BLOCK = 512
EPS = 1e-6
LANES = 128
SUBLANES = 8
ROW_TILES = D_MODEL // LANES
PACK_TILES = ROW_TILES // 2
N_PROJ = 7
CONV_HALO = 32
VMEM_LIMIT = 48 * 1024 * 1024


def _cparams(sem):
    return pltpu.CompilerParams(dimension_semantics=sem, vmem_limit_bytes=VMEM_LIMIT)


def _rope(t, c, s_lo, s_hi):
    half = ROT_DIM // 2
    n = t.shape[-1]
    return t * c + pltpu.roll(t, n - half, 1) * s_lo + pltpu.roll(t, half, 1) * s_hi


def _inproj_body(x_ref, g_ref, w_ref, rc_ref, rlo_ref, rhi_ref,
                 q_ref, k_ref, v_ref, z_ref, ga_ref, gc_ref):
    x = x_ref[...]
    ms = jnp.mean(x * x, axis=-1, keepdims=True)
    xn = ((x * lax.rsqrt(ms + EPS)) * g_ref[...]).astype(BF16)

    def proj(j):
        return jnp.dot(xn, w_ref[:, j * D_MODEL:(j + 1) * D_MODEL], preferred_element_type=F32)

    reps = (1, D_MODEL // LANES)
    c, lo, hi = jnp.tile(rc_ref[...], reps), jnp.tile(rlo_ref[...], reps), jnp.tile(rhi_ref[...], reps)
    q_ref[...] = (_rope(proj(0), c, lo, hi) * (HEAD_DIM ** -0.5 * math.log2(math.e))).astype(BF16)
    k_ref[...] = _rope(proj(1), c, lo, hi).astype(BF16)
    v_ref[...] = proj(2).astype(BF16)
    z_ref[...] = (proj(3) * jax.nn.sigmoid(proj(4))).astype(BF16)
    ga_ref[...] = jax.nn.sigmoid(proj(5)).astype(BF16)
    gc_ref[...] = jax.nn.sigmoid(proj(6)).astype(BF16)


def _inproj(x2d, g, w_bf16, rc, rlo, rhi, tm):
    rows = x2d.shape[0]
    tab_blocks = rc.shape[0] // tm
    row_spec = pl.BlockSpec((tm, D_MODEL), lambda i: (i, 0))
    tab_spec = pl.BlockSpec((tm, LANES), lambda i: (i % tab_blocks, 0))
    out = jax.ShapeDtypeStruct((rows, D_MODEL), BF16)
    return pl.pallas_call(
        _inproj_body,
        grid=(rows // tm,),
        in_specs=[row_spec,
                  pl.BlockSpec((1, D_MODEL), lambda i: (0, 0)),
                  pl.BlockSpec((D_MODEL, N_PROJ * D_MODEL), lambda i: (0, 0), pipeline_mode=pl.Buffered(1)),
                  tab_spec, tab_spec, tab_spec],
        out_specs=[row_spec] * 6,
        out_shape=[out] * 6,
        compiler_params=_cparams(("arbitrary",)),
        name="inproj",
    )(x2d, g, w_bf16, rc, rlo, rhi)


def _rope_tables(length):
    half = ROT_DIM // 2
    inv_freq = np.float32(ROPE_THETA) ** (-np.arange(0, ROT_DIM, 2, dtype=np.float32) / np.float32(ROT_DIM))
    ang = (np.arange(length, dtype=np.float32)[:, None] * inv_freq[None, :]).astype(np.float32)
    cos, sin = np.cos(ang).astype(np.float32), np.sin(ang).astype(np.float32)
    ones = np.ones((length, HEAD_DIM - ROT_DIM), np.float32)
    c = np.concatenate([cos, cos, ones], axis=1)
    lo = np.concatenate([-sin, np.zeros((length, HEAD_DIM - half), np.float32)], axis=1)
    hi = np.concatenate([np.zeros((length, half), np.float32), sin,
                         np.zeros((length, HEAD_DIM - ROT_DIM), np.float32)], axis=1)
    rep = LANES // HEAD_DIM
    return tuple(jnp.asarray(np.tile(t, (1, rep))) for t in (c, lo, hi))


def _dot_nt(a, b):
    return lax.dot_general(a, b, (((1,), (1,)), ((), ())), preferred_element_type=F32)


def _attn_body(lq1_ref, lk1_ref, lq2_ref, lk2_ref, sg_ref, q_ref, k_ref, v_ref, km_ref, vm_ref,
               o_ref, *, seq, tq, tk_max, lam_init):
    lam = (jnp.exp(jnp.sum(lq1_ref[...] * lk1_ref[...], keepdims=True))
           - jnp.exp(jnp.sum(lq2_ref[...] * lk2_ref[...], keepdims=True)) + lam_init)
    lane = lax.broadcasted_iota(jnp.int32, (tq, LANES), 1)
    lane2 = lax.broadcasted_iota(jnp.int32, (2 * tq, LANES), 1)
    neg = -jnp.inf
    km = km_ref[...]
    vm = vm_ref[...]

    for qi in range(seq // tq):
        q = q_ref[qi * tq:(qi + 1) * tq, :]
        zero = jnp.zeros_like(q)
        q2 = jnp.concatenate([jnp.where(lane < HEAD_DIM, q, zero), jnp.where(lane >= HEAD_DIM, q, zero)], axis=0)
        n_keys = (qi + 1) * tq
        blocks = [(start, min(tk_max, n_keys - start)) for start in range(0, n_keys, tk_max)]

        def scores(start, width):
            s = _dot_nt(q2, k_ref[start:start + width, :])
            if start + width > qi * tq:
                row_chunk = (qi * tq + lax.broadcasted_iota(jnp.int32, s.shape, 0) % tq) // CHUNK
                col_chunk = (start + lax.broadcasted_iota(jnp.int32, s.shape, 1)) // CHUNK
                s = jnp.where(col_chunk <= row_chunk, s, neg)
            tiles = [s[:, t * LANES:(t + 1) * LANES] for t in range(width // LANES)]
            if start == 0:
                tiles.append(jnp.where(lane2 < N_META, _dot_nt(q2, km), neg))
            return tiles

        m = l = acc = None
        tiles_next = scores(*blocks[0])
        for bi, (start, width) in enumerate(blocks):
            tiles = tiles_next
            if bi + 1 < len(blocks):
                tiles_next = scores(*blocks[bi + 1])
            tile_max = functools.reduce(jnp.maximum, tiles)
            m_blk = jnp.broadcast_to(jnp.max(tile_max, axis=-1, keepdims=True), tile_max.shape)
            m_new = m_blk if m is None else jnp.maximum(m, m_blk)
            p_tiles = [jnp.exp2(t - m_new) for t in tiles]
            n_real = width // LANES
            p_sum = functools.reduce(jnp.add, p_tiles)
            pv = jnp.dot(jnp.concatenate(p_tiles[:n_real], axis=1).astype(BF16), v_ref[start:start + width, :],
                         preferred_element_type=F32)
            if start == 0:
                pv = pv + jnp.dot(p_tiles[n_real].astype(BF16), vm, preferred_element_type=F32)
            if m is None:
                l, acc = p_sum, pv
            else:
                alpha = jnp.exp2(m - m_new)
                l = alpha * l + p_sum
                acc = alpha * acc + pv
            m = m_new

        o_maps = acc / jnp.sum(l, axis=-1, keepdims=True)
        o = o_maps[:tq] - lam * o_maps[tq:]
        ms = jnp.mean(o * o, axis=-1, keepdims=True)
        o = ((o * lax.rsqrt(ms + EPS)) * sg_ref[...]) * (1.0 - lam_init)
        o_ref[qi * tq:(qi + 1) * tq, :] = o.astype(BF16)


def _attention(q, k, v, k_meta, v_meta, lq1, lk1, lq2, lk2, subln_g, batch, seq, lam_init):
    tq = min(256, seq)
    head_spec = pl.BlockSpec((seq, LANES), lambda b, h: (b, h))
    meta_spec = pl.BlockSpec((LANES, LANES), lambda b, h: (0, h))
    lam_spec = pl.BlockSpec((1, HEAD_DIM), lambda b, h: (0, 0))
    pad = jnp.zeros((LANES - N_META, k_meta.shape[1]), k_meta.dtype)
    k_meta = jnp.concatenate([k_meta, pad], axis=0)
    v_meta = jnp.concatenate([v_meta, pad], axis=0)
    body = functools.partial(_attn_body, seq=seq, tq=tq, tk_max=512, lam_init=lam_init)
    return pl.pallas_call(
        body,
        grid=(batch, N_HEADS),
        in_specs=[lam_spec] * 4 + [pl.BlockSpec((1, LANES), lambda b, h: (0, 0)),
                                   head_spec, head_spec, head_spec, meta_spec, meta_spec],
        out_specs=head_spec,
        out_shape=jax.ShapeDtypeStruct(q.shape, BF16),
        compiler_params=_cparams(("arbitrary", "arbitrary")),
        name="attn",
    )(lq1, lk1, lq2, lk2, subln_g, q, k, v, k_meta, v_meta)


def _post_body(o_ref, z_ref, zprev_ref, zmeta_ref, ga_ref, gc_ref, x_ref,
               wo_ref, wpw_ref, wout_ref, cw_ref, cb_ref, lng_ref, lnb_ref, n2g_ref, wr_ref, br_ref,
               hres_ref, h2_ref, route_ref, zext_ref, zph_ref, conv_ref, zc_ref, *, tm, tiles_per_seq, conv_rows):
    i = pl.program_id(0)
    first = (i % tiles_per_seq) == 0
    meta_halo = jnp.concatenate(
        [jnp.zeros((CONV_HALO - N_META, D_MODEL), F32), zmeta_ref[...].astype(F32)], axis=0)
    zext_ref[0:CONV_HALO, :] = jnp.where(first, meta_halo, zprev_ref[...].astype(F32))
    zext_ref[CONV_HALO:, :] = z_ref[...].astype(F32)
    ph_rows = zph_ref.shape[1]
    for b in range(1, SUBLANES):
        zph_ref[b - 1] = zext_ref[b:b + ph_rows, :]

    base = CONV_HALO - (CONV_K - 1)
    reps = conv_rows // SUBLANES
    for j in range(D_MODEL // LANES):
        cols = slice(j * LANES, (j + 1) * LANES)
        for c in range(tm // conv_rows):
            r0 = c * conv_rows
            acc = jnp.zeros((conv_rows, LANES), F32)
            for kk in range(CONV_K):
                a, b = divmod(base + kk, SUBLANES)
                lo = r0 + a * SUBLANES
                win = zext_ref[lo:lo + conv_rows, cols] if b == 0 else zph_ref[b - 1, lo:lo + conv_rows, cols]
                acc = acc + win * jnp.tile(cw_ref[kk, :, cols], (reps, 1))
            conv_ref[r0:r0 + conv_rows, cols] = acc

    cb = cb_ref[...]
    lng = lng_ref[...]
    lnb = lnb_ref[...]
    ln_rows = 32
    for c in range(tm // ln_rows):
        r0 = c * ln_rows
        zc = conv_ref[r0:r0 + ln_rows, :] + cb
        mu = jnp.mean(zc, axis=-1, keepdims=True)
        var = jnp.mean(jnp.square(zc - mu), axis=-1, keepdims=True)
        y = ((zc - mu) * lax.rsqrt(var + EPS)) * lng + lnb
        zc_ref[r0:r0 + ln_rows, :] = (y * jax.nn.sigmoid(y)).astype(BF16)

    y_attn = jnp.dot(o_ref[...], wo_ref[...], preferred_element_type=F32)
    y_conv = jnp.dot(zc_ref[...], wpw_ref[...], preferred_element_type=F32)
    mix = ga_ref[...] * y_attn.astype(BF16) + gc_ref[...] * y_conv.astype(BF16)
    h_res = x_ref[...] + jnp.dot(mix, wout_ref[...], preferred_element_type=F32)
    hres_ref[...] = h_res

    ms = jnp.mean(h_res * h_res, axis=-1, keepdims=True)
    h2 = (h_res * lax.rsqrt(ms + EPS)) * n2g_ref[...]
    bits = lax.bitcast_convert_type(h2.astype(BF16).astype(F32), jnp.uint32)
    half = D_MODEL // 2
    packed = bits[:, :half] | (bits[:, half:] >> 16)
    for s in range(PACK_TILES):
        h2_ref[pl.ds(s, tm, stride=PACK_TILES), :] = packed[:, s * LANES:(s + 1) * LANES]

    logits = jnp.dot(h2.astype(BF16), wr_ref[...], preferred_element_type=F32) + br_ref[...]
    lane = lax.broadcasted_iota(jnp.int32, (tm, LANES), 1)
    neg = -jnp.inf
    big = jnp.int32(1 << 20)
    gl = jnp.where((lane >= N_EXPERTS) & (lane < N_EXPERTS + N_GROUPS), logits, neg)
    gmax = jnp.max(gl, axis=-1, keepdims=True)
    gidx = jnp.min(jnp.where(gl == gmax, lane - N_EXPERTS, big), axis=-1, keepdims=True)
    g_w = 1.0 / jnp.sum(jnp.exp(gl - gmax), axis=-1, keepdims=True)
    el = jnp.where((lane < N_EXPERTS) & ((lane // EXPERTS_PER_GROUP) == gidx), logits, neg)
    e1 = jnp.max(el, axis=-1, keepdims=True)
    i1 = jnp.min(jnp.where(el == e1, lane, big), axis=-1, keepdims=True)
    el2 = jnp.where(lane == i1, neg, el)
    e2 = jnp.max(el2, axis=-1, keepdims=True)
    i2 = jnp.min(jnp.where(el2 == e2, lane, big), axis=-1, keepdims=True)
    esum = jnp.sum(jnp.exp(el - e1), axis=-1, keepdims=True)
    p1 = 1.0 / esum
    p2 = jnp.exp(e2 - e1) / esum
    w1 = p1 / (p1 + p2)
    w2 = p2 / (p1 + p2)
    route = jnp.where(lane == 0, g_w * w1, 0.0)
    route = jnp.where(lane == 1, g_w * w2, route)
    route = jnp.where(lane == 2, i1.astype(F32), route)
    route = jnp.where(lane == 3, i2.astype(F32), route)
    route_ref[...] = route


def _post(o, z, z_meta, ga, gc, x2d, wo, wpw, wout, cw, cb, lng, lnb, n2g, wr, br, seq):
    rows = x2d.shape[0]
    tm = min(256, seq)
    tiles_per_seq = seq // tm
    conv_rows = min(64, tm)
    row_spec = pl.BlockSpec((tm, D_MODEL), lambda i: (i, 0))
    full = lambda shape: pl.BlockSpec(shape, lambda i: (0, 0))
    halo_blocks = tm // CONV_HALO
    cw = jnp.broadcast_to(cw[:, None, :], (CONV_K, SUBLANES, D_MODEL))
    body = functools.partial(_post_body, tm=tm, tiles_per_seq=tiles_per_seq, conv_rows=conv_rows)
    return pl.pallas_call(
        body,
        grid=(rows // tm,),
        in_specs=[row_spec, row_spec,
                  pl.BlockSpec((CONV_HALO, D_MODEL), lambda i: (jnp.maximum(i * halo_blocks - 1, 0), 0)),
                  full((N_META, D_MODEL)), row_spec, row_spec, row_spec,
                  full((D_MODEL, D_MODEL)), full((D_MODEL, D_MODEL)), full((D_MODEL, D_MODEL)),
                  pl.BlockSpec((CONV_K, SUBLANES, D_MODEL), lambda i: (0, 0, 0)),
                  full((1, D_MODEL)), full((1, D_MODEL)), full((1, D_MODEL)),
                  full((1, D_MODEL)), full((D_MODEL, LANES)), full((1, LANES))],
        out_specs=[row_spec, pl.BlockSpec((tm * PACK_TILES, LANES), lambda i: (i, 0)),
                   pl.BlockSpec((tm, LANES), lambda i: (i, 0))],
        out_shape=[jax.ShapeDtypeStruct((rows, D_MODEL), F32),
                   jax.ShapeDtypeStruct((rows * PACK_TILES, LANES), jnp.uint32),
                   jax.ShapeDtypeStruct((rows, LANES), F32)],
        scratch_shapes=[pltpu.VMEM((tm + CONV_HALO, D_MODEL), F32),
                        pltpu.VMEM((SUBLANES - 1, tm + CONV_HALO - SUBLANES, D_MODEL), F32),
                        pltpu.VMEM((tm, D_MODEL), F32),
                        pltpu.VMEM((tm, D_MODEL), BF16)],
        compiler_params=_cparams(("arbitrary",)),
        name="post",
    )(o, z, z, z_meta, ga, gc, x2d, wo, wpw, wout, cw, cb, lng, lnb, n2g, wr, br)


ROW_UNROLL = 4


def _tile_rows(start, n_rows=1, tiles=ROW_TILES):
    return pl.ds(pl.multiple_of(start * tiles, tiles), n_rows * tiles)


def _ffn_body(blk_e_ref, nact_ref, tok_ref, tok_next_ref, dst_ref, h2_hbm, wg_ref, wu_ref, wd_ref, y_hbm,
              xg_ref, yb_ref, gsem, ssem):
    i = pl.program_id(0)
    nact = nact_ref[0]
    slot = i % 2

    def start_gather(t_ref, sl):
        def body(j, c):
            for pr in range(2):
                r = 2 * j + pr
                pltpu.make_async_copy(h2_hbm.at[_tile_rows(t_ref[0, 0, r], 1, PACK_TILES), :],
                                      xg_ref.at[sl, _tile_rows(r, 1, PACK_TILES), :], gsem.at[sl]).start(priority=pr)
            return c
        lax.fori_loop(0, MOE_BLOCK // 2, body, 0, unroll=ROW_UNROLL)

    def wait_gather(sl):
        pltpu.make_async_copy(h2_hbm.at[_tile_rows(0, MOE_BLOCK, PACK_TILES), :], xg_ref.at[sl], gsem.at[sl]).wait()

    def start_scatter():
        def body(j, c):
            for pr in range(2):
                r = 2 * j + pr
                pltpu.make_async_copy(yb_ref.at[_tile_rows(r), :], y_hbm.at[_tile_rows(dst_ref[0, 0, r]), :],
                                      ssem).start(priority=pr)
            return c
        lax.fori_loop(0, MOE_BLOCK // 2, body, 0, unroll=ROW_UNROLL)

    def wait_scatter():
        pltpu.make_async_copy(yb_ref, y_hbm.at[_tile_rows(0, MOE_BLOCK), :], ssem).wait()

    @pl.when(i == 0)
    def _():
        yb_ref[...] = jnp.zeros_like(yb_ref)
        n_rows = y_hbm.shape[0] // ROW_TILES
        spare = pltpu.make_async_copy(yb_ref, y_hbm.at[_tile_rows(n_rows - MOE_BLOCK, MOE_BLOCK), :], ssem)
        spare.start()
        spare.wait()

    @pl.when((i == 0) & (nact > 0))
    def _():
        start_gather(tok_ref, 0)

    @pl.when(i < nact)
    def _():
        wait_gather(slot)

        @pl.when(i + 1 < nact)
        def _():
            start_gather(tok_next_ref, 1 - slot)

        words = jnp.concatenate([xg_ref[slot, pl.ds(s, MOE_BLOCK, stride=PACK_TILES), :] for s in range(PACK_TILES)],
                                axis=1)
        x_hi = lax.bitcast_convert_type(words & jnp.uint32(0xFFFF0000), F32)
        x_lo = lax.bitcast_convert_type(words << 16, F32)
        xb = jnp.concatenate([x_hi, x_lo], axis=1).astype(BF16)
        g = jnp.dot(xb, wg_ref[0], preferred_element_type=F32)
        u = jnp.dot(xb, wu_ref[0], preferred_element_type=F32)
        hmid = ((g * jax.nn.sigmoid(g)) * u).astype(BF16)
        y = jnp.dot(hmid, wd_ref[0], preferred_element_type=F32)

        @pl.when(i > 0)
        def _():
            wait_scatter()

        for s in range(ROW_TILES):
            yb_ref[pl.ds(s, MOE_BLOCK, stride=ROW_TILES), :] = y[:, s * LANES:(s + 1) * LANES]
        start_scatter()

        @pl.when(i == nact - 1)
        def _():
            wait_scatter()


def _ffn(blk_e, nact, tok, dst, h2_tiles, wg, wu, wd, n_out_rows):
    nb = blk_e.shape[0]
    smem_block = (1, 1, MOE_BLOCK)
    cur = pl.BlockSpec(smem_block, lambda i, be, na: (i, 0, 0), memory_space=pltpu.SMEM)
    nxt = pl.BlockSpec(smem_block, lambda i, be, na: (jnp.minimum(i + 1, nb - 1), 0, 0), memory_space=pltpu.SMEM)
    grid_spec = pltpu.PrefetchScalarGridSpec(
        num_scalar_prefetch=2,
        grid=(nb,),
        in_specs=[cur, nxt, cur,
                  pl.BlockSpec(memory_space=pl.ANY),
                  pl.BlockSpec((1, D_MODEL, D_EXPERT), lambda i, be, na: (be[i], 0, 0)),
                  pl.BlockSpec((1, D_MODEL, D_EXPERT), lambda i, be, na: (be[i], 0, 0)),
                  pl.BlockSpec((1, D_EXPERT, D_MODEL), lambda i, be, na: (be[i], 0, 0))],
        out_specs=pl.BlockSpec(memory_space=pl.ANY),
        scratch_shapes=[pltpu.VMEM((2, MOE_BLOCK * PACK_TILES, LANES), jnp.uint32),
                        pltpu.VMEM((MOE_BLOCK * ROW_TILES, LANES), F32),
                        pltpu.SemaphoreType.DMA((2,)), pltpu.SemaphoreType.DMA(())],
    )
    return pl.pallas_call(
        _ffn_body,
        grid_spec=grid_spec,
        out_shape=jax.ShapeDtypeStruct((n_out_rows * ROW_TILES, LANES), F32),
        compiler_params=_cparams(("arbitrary",)),
        name="ffn",
    )(blk_e, nact, tok, tok, dst, h2_tiles, wg, wu, wd)


def _dispatch(route, n_tok):
    eid = route[:, 2:2 + TOP_K].astype(jnp.int32)
    n_assign = n_tok * TOP_K
    flat_e = eid.reshape(n_assign)
    order = jnp.argsort(flat_e, stable=True).astype(jnp.int32)
    experts = jnp.arange(N_EXPERTS, dtype=jnp.int32)
    counts = jnp.sum((flat_e[None, :] == experts[:, None]).astype(jnp.int32), axis=1)
    starts = jnp.cumsum(counts) - counts
    pcounts = (counts + MOE_BLOCK - 1) // MOE_BLOCK * MOE_BLOCK
    pends = jnp.cumsum(pcounts)
    pstarts = pends - pcounts
    nb = -(-n_assign // MOE_BLOCK) + N_EXPERTS
    blk_start = jnp.arange(nb, dtype=jnp.int32) * MOE_BLOCK
    blk_e = jnp.minimum(jnp.sum((pends[None, :] <= blk_start[:, None]).astype(jnp.int32), axis=1), N_EXPERTS - 1)
    blk_cnt = jnp.clip(counts[blk_e] - (blk_start - pstarts[blk_e]), 0, MOE_BLOCK).astype(jnp.int32)
    within = jnp.arange(MOE_BLOCK, dtype=jnp.int32)[None, :]
    valid = within < blk_cnt[:, None]
    src = jnp.clip((starts[blk_e] + blk_start - pstarts[blk_e])[:, None] + within, 0, n_assign - 1)
    assign = order[src]
    tok = jnp.where(valid, assign // TOP_K, 0).astype(jnp.int32)
    dst = jnp.where(valid, (assign % TOP_K) * n_tok + assign // TOP_K, TOP_K * n_tok + within).astype(jnp.int32)
    n_active = (pends[-1:] // MOE_BLOCK).astype(jnp.int32)
    return blk_e, n_active, tok.reshape(nb, 1, MOE_BLOCK), dst.reshape(nb, 1, MOE_BLOCK)


def _final_body(hres_ref, y0_ref, y1_ref, route_ref, g_ref, out_ref, *, tm):
    y0 = jnp.concatenate([y0_ref[pl.ds(s, tm, stride=ROW_TILES), :] for s in range(ROW_TILES)], axis=1)
    y1 = jnp.concatenate([y1_ref[pl.ds(s, tm, stride=ROW_TILES), :] for s in range(ROW_TILES)], axis=1)
    route = route_ref[...]
    h = hres_ref[...] + (y0 * route[:, 0:1] + y1 * route[:, 1:2])
    ms = jnp.mean(h * h, axis=-1, keepdims=True)
    out_ref[...] = (h * lax.rsqrt(ms + EPS)) * g_ref[...]


def _final(h_res, y_tiles, route, final_g, seq):
    rows = h_res.shape[0]
    tm = min(512, seq)
    nt = rows // tm
    row_spec = pl.BlockSpec((tm, D_MODEL), lambda i: (i, 0))
    return pl.pallas_call(
        functools.partial(_final_body, tm=tm),
        grid=(nt,),
        in_specs=[row_spec,
                  pl.BlockSpec((tm * ROW_TILES, LANES), lambda i: (i, 0)),
                  pl.BlockSpec((tm * ROW_TILES, LANES), lambda i: (i + nt, 0)),
                  pl.BlockSpec((tm, LANES), lambda i: (i, 0)),
                  pl.BlockSpec((1, D_MODEL), lambda i: (0, 0))],
        out_specs=row_spec,
        out_shape=jax.ShapeDtypeStruct((rows, D_MODEL), F32),
        compiler_params=_cparams(("arbitrary",)),
        name="final",
    )(h_res, y_tiles, y_tiles, route, final_g)


def kernel(x, meta_tokens, norm1_g, w_in, lam_q1, lam_k1, lam_q2, lam_k2, subln_g, w_o_attn, conv_w, conv_b,
           conv_ln_g, conv_ln_b, w_pw2, w_out, norm2_g, w_group, b_group, w_router, b_router, w_e_gate,
           w_e_up, w_e_down, final_g):
    batch, seq, _ = x.shape
    n_tok = batch * seq
    layer = 0
    lam_init = 0.8 - 0.6 * math.exp(-0.3 * layer)
    row = lambda a: a.reshape(1, -1).astype(F32)

    x2d = x.reshape(n_tok, D_MODEL)
    w_in_b = w_in[layer].astype(BF16)
    rc, rlo, rhi = _rope_tables(N_META + seq)
    tm_in = min(512, seq)
    q, k, v, z, ga, gc = _inproj(x2d, row(norm1_g[layer]), w_in_b, rc[N_META:], rlo[N_META:], rhi[N_META:], tm_in)
    _, k_meta, v_meta, z_meta, _, _ = _inproj(meta_tokens.astype(F32), row(norm1_g[layer]), w_in_b,
                                              rc[:N_META], rlo[:N_META], rhi[:N_META], N_META)

    o = _attention(q, k, v, k_meta, v_meta, row(lam_q1[layer]), row(lam_k1[layer]), row(lam_q2[layer]),
                   row(lam_k2[layer]), row(subln_g[layer]), batch, seq, lam_init)

    cw = conv_w[layer].astype(F32)
    pad_cols = LANES - N_EXPERTS - N_GROUPS
    wr = jnp.concatenate([w_router[layer].astype(F32), w_group[layer].astype(F32),
                          jnp.zeros((D_MODEL, pad_cols), F32)], axis=1).astype(BF16)
    br = jnp.concatenate([b_router[layer].astype(F32), b_group[layer].astype(F32),
                          jnp.zeros((pad_cols,), F32)]).reshape(1, LANES)
    h_res, h2_tiles, route = _post(o, z, z_meta, ga, gc, x2d, w_o_attn[layer].astype(BF16),
                                   w_pw2[layer].astype(BF16), w_out[layer].astype(BF16), cw,
                                   row(conv_b[layer]), row(conv_ln_g[layer]), row(conv_ln_b[layer]),
                                   row(norm2_g[layer]), wr, br, seq)

    blk_e, n_active, tok, dst = _dispatch(route, n_tok)
    y_tiles = _ffn(blk_e, n_active, tok, dst, h2_tiles, w_e_gate[layer].astype(BF16), w_e_up[layer].astype(BF16),
                   w_e_down[layer].astype(BF16), TOP_K * n_tok + MOE_BLOCK)

    out = _final(h_res, y_tiles, route, row(final_g), seq)
    return out.reshape(batch, seq, D_MODEL)
```

```python
import functools
import math

import jax
import jax.numpy as jnp
import numpy as np
from jax import lax
from jax.experimental import pallas as pl
from jax.experimental.pallas import tpu as pltpu

F32 = jnp.float32
BF16 = jnp.bfloat16

D_MODEL = 1024
CHUNK = 64
N_META = 16
N_HEADS = 8
HEAD_DIM = 64
ROT_DIM = HEAD_DIM // 4
ROPE_THETA = 500000.0
ATTN_W = N_HEADS * 2 * HEAD_DIM
CONV_K = 31
N_GROUPS = 4
EXPERTS_PER_GROUP = 8
N_EXPERTS = N_GROUPS * EXPERTS_PER_GROUP
TOP_K = 2
D_EXPERT = D_MODEL // 2
MOE_BLOCK = 512
EPS = 1e-6
LANES = 128
SUBLANES = 8
ROW_TILES = D_MODEL // LANES
PACK_TILES = ROW_TILES // 2
N_PROJ = 7
CONV_HALO = 32
VMEM_LIMIT = 48 * 1024 * 1024
VMEM_LIMIT_POST = 56 * 1024 * 1024


def _cparams(sem, vmem_limit=VMEM_LIMIT):
    return pltpu.CompilerParams(dimension_semantics=sem, vmem_limit_bytes=vmem_limit)


def _rope(t, c, s_lo, s_hi):
    half = ROT_DIM // 2
    n = t.shape[-1]
    return t * c + pltpu.roll(t, n - half, 1) * s_lo + pltpu.roll(t, half, 1) * s_hi


def _inproj_body(x_ref, g_ref, w_ref, rc_ref, rlo_ref, rhi_ref,
                 q_ref, k_ref, v_ref, z_ref, ga_ref, gc_ref):
    tm = x_ref.shape[0]
    n_half = 2 if tm % 32 == 0 else 1
    reps = (1, D_MODEL // LANES)
    for h in range(n_half):
        rows = slice(h * (tm // n_half), (h + 1) * (tm // n_half))
        x = x_ref[rows, :]
        ms = jnp.mean(x * x, axis=-1, keepdims=True)
        xn = ((x * lax.rsqrt(ms + EPS)) * g_ref[...]).astype(BF16)

        def proj(j):
            return jnp.dot(xn, w_ref[:, j * D_MODEL:(j + 1) * D_MODEL], preferred_element_type=F32)

        c, lo, hi = jnp.tile(rc_ref[rows, :], reps), jnp.tile(rlo_ref[rows, :], reps), jnp.tile(rhi_ref[rows, :], reps)
        q_ref[rows, :] = (_rope(proj(0), c, lo, hi) * (HEAD_DIM ** -0.5 * math.log2(math.e))).astype(BF16)
        k_ref[rows, :] = _rope(proj(1), c, lo, hi).astype(BF16)
        v_ref[rows, :] = proj(2).astype(BF16)
        z_ref[rows, :] = (proj(3) * jax.nn.sigmoid(proj(4))).astype(BF16)
        ga_ref[rows, :] = jax.nn.sigmoid(proj(5)).astype(BF16)
        gc_ref[rows, :] = jax.nn.sigmoid(proj(6)).astype(BF16)


def _inproj(x2d, g, w_bf16, rc, rlo, rhi, tm):
    rows = x2d.shape[0]
    tab_blocks = rc.shape[0] // tm
    row_spec = pl.BlockSpec((tm, D_MODEL), lambda i: (i, 0))
    tab_spec = pl.BlockSpec((tm, LANES), lambda i: (i % tab_blocks, 0))
    out = jax.ShapeDtypeStruct((rows, D_MODEL), BF16)
    return pl.pallas_call(
        _inproj_body,
        grid=(rows // tm,),
        in_specs=[row_spec,
                  pl.BlockSpec((1, D_MODEL), lambda i: (0, 0)),
                  pl.BlockSpec((D_MODEL, N_PROJ * D_MODEL), lambda i: (0, 0), pipeline_mode=pl.Buffered(1)),
                  tab_spec, tab_spec, tab_spec],
        out_specs=[row_spec] * 6,
        out_shape=[out] * 6,
        compiler_params=_cparams(("arbitrary",)),
        name="inproj",
    )(x2d, g, w_bf16, rc, rlo, rhi)


def _rope_tables(length):
    half = ROT_DIM // 2
    inv_freq = np.float32(ROPE_THETA) ** (-np.arange(0, ROT_DIM, 2, dtype=np.float32) / np.float32(ROT_DIM))
    ang = (np.arange(length, dtype=np.float32)[:, None] * inv_freq[None, :]).astype(np.float32)
    cos, sin = np.cos(ang).astype(np.float32), np.sin(ang).astype(np.float32)
    ones = np.ones((length, HEAD_DIM - ROT_DIM), np.float32)
    c = np.concatenate([cos, cos, ones], axis=1)
    lo = np.concatenate([-sin, np.zeros((length, HEAD_DIM - half), np.float32)], axis=1)
    hi = np.concatenate([np.zeros((length, half), np.float32), sin,
                         np.zeros((length, HEAD_DIM - ROT_DIM), np.float32)], axis=1)
    rep = LANES // HEAD_DIM
    return tuple(jnp.asarray(np.tile(t, (1, rep))) for t in (c, lo, hi))


def _dot_nt(a, b):
    return lax.dot_general(a, b, (((1,), (1,)), ((), ())), preferred_element_type=F32)


def _attn_body(lq1_ref, lk1_ref, lq2_ref, lk2_ref, sg_ref, q_ref, k_ref, v_ref, km_ref, vm_ref,
               o_ref, *, seq, tq, tk_max, lam_init):
    lam = (jnp.exp(jnp.sum(lq1_ref[...] * lk1_ref[...], keepdims=True))
           - jnp.exp(jnp.sum(lq2_ref[...] * lk2_ref[...], keepdims=True)) + lam_init)
    lane = lax.broadcasted_iota(jnp.int32, (tq, LANES), 1)
    lane2 = lax.broadcasted_iota(jnp.int32, (2 * tq, LANES), 1)
    neg = -jnp.inf
    km = km_ref[...]
    vm = vm_ref[...]
    vm_ext = jnp.concatenate([vm, (lax.broadcasted_iota(jnp.int32, vm.shape, 0) < N_META).astype(BF16)], axis=1)

    for qi in range(seq // tq):
        q = q_ref[qi * tq:(qi + 1) * tq, :]
        zero = jnp.zeros_like(q)
        q2 = jnp.concatenate([jnp.where(lane < HEAD_DIM, q, zero), jnp.where(lane >= HEAD_DIM, q, zero)], axis=0)
        n_keys = (qi + 1) * tq
        blocks = [(start, min(tk_max, n_keys - start)) for start in range(0, n_keys, tk_max)]

        def scores(start, width):
            s = _dot_nt(q2, k_ref[start:start + width, :])
            if start + width > qi * tq:
                row_chunk = (qi * tq + lax.broadcasted_iota(jnp.int32, s.shape, 0) % tq) // CHUNK
                col_chunk = (start + lax.broadcasted_iota(jnp.int32, s.shape, 1)) // CHUNK
                s = jnp.where(col_chunk <= row_chunk, s, neg)
            tiles = [s[:, t * LANES:(t + 1) * LANES] for t in range(width // LANES)]
            if start == 0:
                tiles.append(jnp.where(lane2 < N_META, _dot_nt(q2, km), neg))
            return tiles

        m = acc = None
        tiles_next = scores(*blocks[0])
        for bi, (start, width) in enumerate(blocks):
            tiles = tiles_next
            if bi + 1 < len(blocks):
                tiles_next = scores(*blocks[bi + 1])
            tile_max = functools.reduce(jnp.maximum, tiles)
            m_blk = jnp.broadcast_to(jnp.max(tile_max, axis=-1, keepdims=True), tile_max.shape)
            m_new = m_blk if m is None else jnp.maximum(m, m_blk)
            p_tiles = [jnp.exp2(t - m_new) for t in tiles]
            n_real = width // LANES
            vb = v_ref[start:start + width, :]
            v_ext = jnp.concatenate([vb, jnp.ones_like(vb)], axis=1)
            pv = jnp.dot(jnp.concatenate(p_tiles[:n_real], axis=1).astype(BF16), v_ext,
                         preferred_element_type=F32)
            if start == 0:
                pv = pv + jnp.dot(p_tiles[n_real].astype(BF16), vm_ext, preferred_element_type=F32)
            if m is None:
                acc = pv
            else:
                alpha = jnp.exp2(m - m_new)
                acc = jnp.concatenate([alpha, alpha], axis=1) * acc + pv
            m = m_new

        o_maps = acc[:, :LANES] / acc[:, LANES:]
        o = o_maps[:tq] - lam * o_maps[tq:]
        ms = jnp.mean(o * o, axis=-1, keepdims=True)
        o = ((o * lax.rsqrt(ms + EPS)) * sg_ref[...]) * (1.0 - lam_init)
        o_ref[qi * tq:(qi + 1) * tq, :] = o.astype(BF16)


def _attention(q, k, v, k_meta, v_meta, lq1, lk1, lq2, lk2, subln_g, batch, seq, lam_init):
    tq = min(256, seq)
    head_spec = pl.BlockSpec((seq, LANES), lambda b, h: (b, h))
    meta_spec = pl.BlockSpec((LANES, LANES), lambda b, h: (0, h))
    lam_spec = pl.BlockSpec((1, HEAD_DIM), lambda b, h: (0, 0))
    pad = jnp.zeros((LANES - N_META, k_meta.shape[1]), k_meta.dtype)
    k_meta = jnp.concatenate([k_meta, pad], axis=0)
    v_meta = jnp.concatenate([v_meta, pad], axis=0)
    body = functools.partial(_attn_body, seq=seq, tq=tq, tk_max=512, lam_init=lam_init)
    return pl.pallas_call(
        body,
        grid=(batch, N_HEADS),
        in_specs=[lam_spec] * 4 + [pl.BlockSpec((1, LANES), lambda b, h: (0, 0)),
                                   head_spec, head_spec, head_spec, meta_spec, meta_spec],
        out_specs=head_spec,
        out_shape=jax.ShapeDtypeStruct(q.shape, BF16),
        compiler_params=_cparams(("arbitrary", "arbitrary")),
        name="attn",
    )(lq1, lk1, lq2, lk2, subln_g, q, k, v, k_meta, v_meta)


def _post_body(o_ref, z_ref, zprev_ref, zmeta_ref, ga_ref, gc_ref, x_ref,
               wo_ref, wpw_ref, wout_ref, cw_ref, cb_ref, lng_ref, lnb_ref, n2g_ref, wr_ref, br_ref,
               hres_ref, h2_ref, route_ref, zext_ref, zph_ref, conv_ref, zc_ref, *, tm, tiles_per_seq, conv_rows):
    i = pl.program_id(0)
    first = (i % tiles_per_seq) == 0
    meta_halo = jnp.concatenate(
        [jnp.zeros((CONV_HALO - N_META, D_MODEL), F32), zmeta_ref[...].astype(F32)], axis=0)
    zext_ref[0:CONV_HALO, :] = jnp.where(first, meta_halo, zprev_ref[...].astype(F32))
    zext_ref[CONV_HALO:, :] = z_ref[...].astype(F32)
    ph_rows = zph_ref.shape[1]
    for b in range(1, SUBLANES):
        zph_ref[b - 1] = zext_ref[b:b + ph_rows, :]

    base = CONV_HALO - (CONV_K - 1)
    reps = conv_rows // SUBLANES
    for j in range(D_MODEL // LANES):
        cols = slice(j * LANES, (j + 1) * LANES)
        for c in range(tm // conv_rows):
            r0 = c * conv_rows
            acc = jnp.zeros((conv_rows, LANES), F32)
            for kk in range(CONV_K):
                a, b = divmod(base + kk, SUBLANES)
                lo = r0 + a * SUBLANES
                win = zext_ref[lo:lo + conv_rows, cols] if b == 0 else zph_ref[b - 1, lo:lo + conv_rows, cols]
                acc = acc + win * jnp.tile(cw_ref[kk, :, cols], (reps, 1))
            conv_ref[r0:r0 + conv_rows, cols] = acc

    cb = cb_ref[...]
    lng = lng_ref[...]
    lnb = lnb_ref[...]
    ln_rows = 32
    for c in range(tm // ln_rows):
        r0 = c * ln_rows
        zc = conv_ref[r0:r0 + ln_rows, :] + cb
        mu = jnp.mean(zc, axis=-1, keepdims=True)
        var = jnp.mean(jnp.square(zc - mu), axis=-1, keepdims=True)
        y = ((zc - mu) * lax.rsqrt(var + EPS)) * lng + lnb
        zc_ref[r0:r0 + ln_rows, :] = (y * jax.nn.sigmoid(y)).astype(BF16)

    y_attn = jnp.dot(o_ref[...], wo_ref[...], preferred_element_type=F32)
    y_conv = jnp.dot(zc_ref[...], wpw_ref[...], preferred_element_type=F32)
    mix = ga_ref[...] * y_attn.astype(BF16) + gc_ref[...] * y_conv.astype(BF16)
    h_res = x_ref[...] + jnp.dot(mix, wout_ref[...], preferred_element_type=F32)
    hres_ref[...] = h_res

    ms = jnp.mean(h_res * h_res, axis=-1, keepdims=True)
    h2 = (h_res * lax.rsqrt(ms + EPS)) * n2g_ref[...]
    bits = lax.bitcast_convert_type(h2.astype(BF16).astype(F32), jnp.uint32)
    half = D_MODEL // 2
    packed = bits[:, :half] | (bits[:, half:] >> 16)
    for s in range(PACK_TILES):
        h2_ref[pl.ds(s, tm, stride=PACK_TILES), :] = packed[:, s * LANES:(s + 1) * LANES]

    logits = jnp.dot(h2.astype(BF16), wr_ref[...], preferred_element_type=F32) + br_ref[...]
    lane = lax.broadcasted_iota(jnp.int32, (tm, LANES), 1)
    neg = -jnp.inf
    big = jnp.int32(1 << 20)
    gl = jnp.where((lane >= N_EXPERTS) & (lane < N_EXPERTS + N_GROUPS), logits, neg)
    gmax = jnp.max(gl, axis=-1, keepdims=True)
    gidx = jnp.min(jnp.where(gl == gmax, lane - N_EXPERTS, big), axis=-1, keepdims=True)
    g_w = 1.0 / jnp.sum(jnp.exp(gl - gmax), axis=-1, keepdims=True)
    el = jnp.where((lane < N_EXPERTS) & ((lane // EXPERTS_PER_GROUP) == gidx), logits, neg)
    e1 = jnp.max(el, axis=-1, keepdims=True)
    i1 = jnp.min(jnp.where(el == e1, lane, big), axis=-1, keepdims=True)
    el2 = jnp.where(lane == i1, neg, el)
    e2 = jnp.max(el2, axis=-1, keepdims=True)
    i2 = jnp.min(jnp.where(el2 == e2, lane, big), axis=-1, keepdims=True)
    esum = jnp.sum(jnp.exp(el - e1), axis=-1, keepdims=True)
    p1 = 1.0 / esum
    p2 = jnp.exp(e2 - e1) / esum
    w1 = p1 / (p1 + p2)
    w2 = p2 / (p1 + p2)
    route = jnp.where(lane == 0, g_w * w1, 0.0)
    route = jnp.where(lane == 1, g_w * w2, route)
    route = jnp.where(lane == 2, i1.astype(F32), route)
    route = jnp.where(lane == 3, i2.astype(F32), route)
    route_ref[...] = route


def _post(o, z, z_meta, ga, gc, x2d, wo, wpw, wout, cw, cb, lng, lnb, n2g, wr, br, seq):
    rows = x2d.shape[0]
    tm = min(512, seq)
    tiles_per_seq = seq // tm
    conv_rows = min(64, tm)
    row_spec = pl.BlockSpec((tm, D_MODEL), lambda i: (i, 0))
    full = lambda shape: pl.BlockSpec(shape, lambda i: (0, 0))
    weight = pl.BlockSpec((D_MODEL, D_MODEL), lambda i: (0, 0), pipeline_mode=pl.Buffered(1))
    halo_blocks = tm // CONV_HALO
    cw = jnp.broadcast_to(cw[:, None, :], (CONV_K, SUBLANES, D_MODEL))
    body = functools.partial(_post_body, tm=tm, tiles_per_seq=tiles_per_seq, conv_rows=conv_rows)
    return pl.pallas_call(
        body,
        grid=(rows // tm,),
        in_specs=[row_spec, row_spec,
                  pl.BlockSpec((CONV_HALO, D_MODEL), lambda i: (jnp.maximum(i * halo_blocks - 1, 0), 0)),
                  full((N_META, D_MODEL)), row_spec, row_spec, row_spec,
                  weight, weight, weight,
                  pl.BlockSpec((CONV_K, SUBLANES, D_MODEL), lambda i: (0, 0, 0)),
                  full((1, D_MODEL)), full((1, D_MODEL)), full((1, D_MODEL)),
                  full((1, D_MODEL)), full((D_MODEL, LANES)), full((1, LANES))],
        out_specs=[row_spec, pl.BlockSpec((tm * PACK_TILES, LANES), lambda i: (i, 0)),
                   pl.BlockSpec((tm, LANES), lambda i: (i, 0))],
        out_shape=[jax.ShapeDtypeStruct((rows, D_MODEL), F32),
                   jax.ShapeDtypeStruct((rows * PACK_TILES, LANES), jnp.uint32),
                   jax.ShapeDtypeStruct((rows, LANES), F32)],
        scratch_shapes=[pltpu.VMEM((tm + CONV_HALO, D_MODEL), F32),
                        pltpu.VMEM((SUBLANES - 1, tm + CONV_HALO - SUBLANES, D_MODEL), F32),
                        pltpu.VMEM((tm, D_MODEL), F32),
                        pltpu.VMEM((tm, D_MODEL), BF16)],
        compiler_params=_cparams(("arbitrary",), VMEM_LIMIT_POST),
        name="post",
    )(o, z, z, z_meta, ga, gc, x2d, wo, wpw, wout, cw, cb, lng, lnb, n2g, wr, br)


ROW_UNROLL = 4


def _tile_rows(start, n_rows=1, tiles=ROW_TILES):
    return pl.ds(pl.multiple_of(start * tiles, tiles), n_rows * tiles)


def _ffn_body(blk_e_ref, nact_ref, tok_ref, tok_next_ref, dst_ref, h2_hbm, wg_ref, wu_ref, wd_ref, y_hbm,
              xg_ref, yb_ref, gsem, ssem):
    i = pl.program_id(0)
    nact = nact_ref[0]
    slot = i % 2

    def start_gather(t_ref, sl):
        def body(j, c):
            for pr in range(2):
                r = 2 * j + pr
                pltpu.make_async_copy(h2_hbm.at[_tile_rows(t_ref[0, 0, r], 1, PACK_TILES), :],
                                      xg_ref.at[sl, _tile_rows(r, 1, PACK_TILES), :], gsem.at[sl]).start(priority=pr)
            return c
        lax.fori_loop(0, MOE_BLOCK // 2, body, 0, unroll=ROW_UNROLL)

    def wait_gather(sl):
        pltpu.make_async_copy(h2_hbm.at[_tile_rows(0, MOE_BLOCK, PACK_TILES), :], xg_ref.at[sl], gsem.at[sl]).wait()

    def start_scatter():
        def body(j, c):
            for pr in range(2):
                r = 2 * j + pr
                pltpu.make_async_copy(yb_ref.at[_tile_rows(r), :], y_hbm.at[_tile_rows(dst_ref[0, 0, r]), :],
                                      ssem).start(priority=pr)
            return c
        lax.fori_loop(0, MOE_BLOCK // 2, body, 0, unroll=ROW_UNROLL)

    def wait_scatter():
        pltpu.make_async_copy(yb_ref, y_hbm.at[_tile_rows(0, MOE_BLOCK), :], ssem).wait()

    @pl.when(i == 0)
    def _():
        yb_ref[...] = jnp.zeros_like(yb_ref)
        n_rows = y_hbm.shape[0] // ROW_TILES
        spare = pltpu.make_async_copy(yb_ref, y_hbm.at[_tile_rows(n_rows - MOE_BLOCK, MOE_BLOCK), :], ssem)
        spare.start()
        spare.wait()

    @pl.when((i == 0) & (nact > 0))
    def _():
        start_gather(tok_ref, 0)

    @pl.when(i < nact)
    def _():
        wait_gather(slot)

        @pl.when(i + 1 < nact)
        def _():
            start_gather(tok_next_ref, 1 - slot)

        words = jnp.concatenate([xg_ref[slot, pl.ds(s, MOE_BLOCK, stride=PACK_TILES), :] for s in range(PACK_TILES)],
                                axis=1)
        x_hi = lax.bitcast_convert_type(words & jnp.uint32(0xFFFF0000), F32)
        x_lo = lax.bitcast_convert_type(words << 16, F32)
        xb = jnp.concatenate([x_hi, x_lo], axis=1).astype(BF16)
        g = jnp.dot(xb, wg_ref[0], preferred_element_type=F32)
        u = jnp.dot(xb, wu_ref[0], preferred_element_type=F32)
        hmid = ((g * jax.nn.sigmoid(g)) * u).astype(BF16)
        y = jnp.dot(hmid, wd_ref[0], preferred_element_type=F32)

        @pl.when(i > 0)
        def _():
            wait_scatter()

        for s in range(ROW_TILES):
            yb_ref[pl.ds(s, MOE_BLOCK, stride=ROW_TILES), :] = y[:, s * LANES:(s + 1) * LANES]
        start_scatter()

        @pl.when(i == nact - 1)
        def _():
            wait_scatter()


def _ffn(blk_e, nact, tok, dst, h2_tiles, wg, wu, wd, n_out_rows):
    nb = blk_e.shape[0]
    smem_block = (1, 1, MOE_BLOCK)
    cur = pl.BlockSpec(smem_block, lambda i, be, na: (i, 0, 0), memory_space=pltpu.SMEM)
    nxt = pl.BlockSpec(smem_block, lambda i, be, na: (jnp.minimum(i + 1, nb - 1), 0, 0), memory_space=pltpu.SMEM)
    grid_spec = pltpu.PrefetchScalarGridSpec(
        num_scalar_prefetch=2,
        grid=(nb,),
        in_specs=[cur, nxt, cur,
                  pl.BlockSpec(memory_space=pl.ANY),
                  pl.BlockSpec((1, D_MODEL, D_EXPERT), lambda i, be, na: (be[i], 0, 0)),
                  pl.BlockSpec((1, D_MODEL, D_EXPERT), lambda i, be, na: (be[i], 0, 0)),
                  pl.BlockSpec((1, D_EXPERT, D_MODEL), lambda i, be, na: (be[i], 0, 0))],
        out_specs=pl.BlockSpec(memory_space=pl.ANY),
        scratch_shapes=[pltpu.VMEM((2, MOE_BLOCK * PACK_TILES, LANES), jnp.uint32),
                        pltpu.VMEM((MOE_BLOCK * ROW_TILES, LANES), F32),
                        pltpu.SemaphoreType.DMA((2,)), pltpu.SemaphoreType.DMA(())],
    )
    return pl.pallas_call(
        _ffn_body,
        grid_spec=grid_spec,
        out_shape=jax.ShapeDtypeStruct((n_out_rows * ROW_TILES, LANES), F32),
        compiler_params=_cparams(("arbitrary",)),
        name="ffn",
    )(blk_e, nact, tok, tok, dst, h2_tiles, wg, wu, wd)


def _dispatch(route, n_tok):
    eid = route[:, 2:2 + TOP_K].astype(jnp.int32)
    n_assign = n_tok * TOP_K
    flat_e = eid.reshape(n_assign)
    order = jnp.argsort(flat_e, stable=True).astype(jnp.int32)
    experts = jnp.arange(N_EXPERTS, dtype=jnp.int32)
    counts = jnp.sum((flat_e[None, :] == experts[:, None]).astype(jnp.int32), axis=1)
    starts = jnp.cumsum(counts) - counts
    pcounts = (counts + MOE_BLOCK - 1) // MOE_BLOCK * MOE_BLOCK
    pends = jnp.cumsum(pcounts)
    pstarts = pends - pcounts
    nb = -(-n_assign // MOE_BLOCK) + N_EXPERTS
    blk_start = jnp.arange(nb, dtype=jnp.int32) * MOE_BLOCK
    blk_e = jnp.minimum(jnp.sum((pends[None, :] <= blk_start[:, None]).astype(jnp.int32), axis=1), N_EXPERTS - 1)
    blk_cnt = jnp.clip(counts[blk_e] - (blk_start - pstarts[blk_e]), 0, MOE_BLOCK).astype(jnp.int32)
    within = jnp.arange(MOE_BLOCK, dtype=jnp.int32)[None, :]
    valid = within < blk_cnt[:, None]
    src = jnp.clip((starts[blk_e] + blk_start - pstarts[blk_e])[:, None] + within, 0, n_assign - 1)
    assign = order[src]
    tok = jnp.where(valid, assign // TOP_K, 0).astype(jnp.int32)
    dst = jnp.where(valid, (assign % TOP_K) * n_tok + assign // TOP_K, TOP_K * n_tok + within).astype(jnp.int32)
    n_active = (pends[-1:] // MOE_BLOCK).astype(jnp.int32)
    return blk_e, n_active, tok.reshape(nb, 1, MOE_BLOCK), dst.reshape(nb, 1, MOE_BLOCK)


def _final_body(hres_ref, y0_ref, y1_ref, route_ref, g_ref, out_ref, *, tm):
    y0 = jnp.concatenate([y0_ref[pl.ds(s, tm, stride=ROW_TILES), :] for s in range(ROW_TILES)], axis=1)
    y1 = jnp.concatenate([y1_ref[pl.ds(s, tm, stride=ROW_TILES), :] for s in range(ROW_TILES)], axis=1)
    route = route_ref[...]
    h = hres_ref[...] + (y0 * route[:, 0:1] + y1 * route[:, 1:2])
    ms = jnp.mean(h * h, axis=-1, keepdims=True)
    out_ref[...] = (h * lax.rsqrt(ms + EPS)) * g_ref[...]


def _final(h_res, y_tiles, route, final_g, seq):
    rows = h_res.shape[0]
    tm = min(512, seq)
    nt = rows // tm
    row_spec = pl.BlockSpec((tm, D_MODEL), lambda i: (i, 0))
    return pl.pallas_call(
        functools.partial(_final_body, tm=tm),
        grid=(nt,),
        in_specs=[row_spec,
                  pl.BlockSpec((tm * ROW_TILES, LANES), lambda i: (i, 0)),
                  pl.BlockSpec((tm * ROW_TILES, LANES), lambda i: (i + nt, 0)),
                  pl.BlockSpec((tm, LANES), lambda i: (i, 0)),
                  pl.BlockSpec((1, D_MODEL), lambda i: (0, 0))],
        out_specs=row_spec,
        out_shape=jax.ShapeDtypeStruct((rows, D_MODEL), F32),
        compiler_params=_cparams(("arbitrary",)),
        name="final",
    )(h_res, y_tiles, y_tiles, route, final_g)


def kernel(x, meta_tokens, norm1_g, w_in, lam_q1, lam_k1, lam_q2, lam_k2, subln_g, w_o_attn, conv_w, conv_b,
           conv_ln_g, conv_ln_b, w_pw2, w_out, norm2_g, w_group, b_group, w_router, b_router, w_e_gate,
           w_e_up, w_e_down, final_g):
    batch, seq, _ = x.shape
    n_tok = batch * seq
    layer = 0
    lam_init = 0.8 - 0.6 * math.exp(-0.3 * layer)
    row = lambda a: a.reshape(1, -1).astype(F32)

    x2d = x.reshape(n_tok, D_MODEL)
    w_in_b = w_in[layer].astype(BF16)
    rc, rlo, rhi = _rope_tables(N_META + seq)
    tm_in = min(512, seq)
    q, k, v, z, ga, gc = _inproj(x2d, row(norm1_g[layer]), w_in_b, rc[N_META:], rlo[N_META:], rhi[N_META:], tm_in)
    _, k_meta, v_meta, z_meta, _, _ = _inproj(meta_tokens.astype(F32), row(norm1_g[layer]), w_in_b,
                                              rc[:N_META], rlo[:N_META], rhi[:N_META], N_META)

    o = _attention(q, k, v, k_meta, v_meta, row(lam_q1[layer]), row(lam_k1[layer]), row(lam_q2[layer]),
                   row(lam_k2[layer]), row(subln_g[layer]), batch, seq, lam_init)

    cw = conv_w[layer].astype(F32)
    pad_cols = LANES - N_EXPERTS - N_GROUPS
    wr = jnp.concatenate([w_router[layer].astype(F32), w_group[layer].astype(F32),
                          jnp.zeros((D_MODEL, pad_cols), F32)], axis=1).astype(BF16)
    br = jnp.concatenate([b_router[layer].astype(F32), b_group[layer].astype(F32),
                          jnp.zeros((pad_cols,), F32)]).reshape(1, LANES)
    h_res, h2_tiles, route = _post(o, z, z_meta, ga, gc, x2d, w_o_attn[layer].astype(BF16),
                                   w_pw2[layer].astype(BF16), w_out[layer].astype(BF16), cw,
                                   row(conv_b[layer]), row(conv_ln_g[layer]), row(conv_ln_b[layer]),
                                   row(norm2_g[layer]), wr, br, seq)

    blk_e, n_active, tok, dst = _dispatch(route, n_tok)
    y_tiles = _ffn(blk_e, n_active, tok, dst, h2_tiles, w_e_gate[layer].astype(BF16), w_e_up[layer].astype(BF16),
                   w_e_down[layer].astype(BF16), TOP_K * n_tok + MOE_BLOCK)

    out = _final(h_res, y_tiles, route, row(final_g), seq)
    return out.reshape(batch, seq, D_MODEL)
```

```python
import functools
import math

import jax
import jax.numpy as jnp
import numpy as np
from jax import lax
from jax.experimental import pallas as pl
from jax.experimental.pallas import tpu as pltpu

F32 = jnp.float32
BF16 = jnp.bfloat16

D_MODEL = 1024
CHUNK = 64
N_META = 16
N_HEADS = 8
HEAD_DIM = 64
ROT_DIM = HEAD_DIM // 4
ROPE_THETA = 500000.0
ATTN_W = N_HEADS * 2 * HEAD_DIM
CONV_K = 31
N_GROUPS = 4
EXPERTS_PER_GROUP = 8
N_EXPERTS = N_GROUPS * EXPERTS_PER_GROUP
TOP_K = 2
D_EXPERT = D_MODEL // 2
MOE_BLOCK = 512
EPS = 1e-6
LANES = 128
SUBLANES = 8
ROW_TILES = D_MODEL // LANES
PACK_TILES = ROW_TILES // 2
N_PROJ = 7
CONV_HALO = 32
VMEM_LIMIT = 48 * 1024 * 1024
VMEM_LIMIT_POST = 56 * 1024 * 1024


def _cparams(sem, vmem_limit=VMEM_LIMIT):
    return pltpu.CompilerParams(dimension_semantics=sem, vmem_limit_bytes=vmem_limit)


def _rope(t, c, s_lo, s_hi):
    half = ROT_DIM // 2
    n = t.shape[-1]
    return t * c + pltpu.roll(t, n - half, 1) * s_lo + pltpu.roll(t, half, 1) * s_hi


def _inproj_body(x_ref, g_ref, w_ref, rc_ref, rlo_ref, rhi_ref,
                 q_ref, k_ref, v_ref, z_ref, ga_ref, gc_ref):
    tm = x_ref.shape[0]
    n_half = 2 if tm % 32 == 0 else 1
    reps = (1, D_MODEL // LANES)
    for h in range(n_half):
        rows = slice(h * (tm // n_half), (h + 1) * (tm // n_half))
        x = x_ref[rows, :]
        ms = jnp.mean(x * x, axis=-1, keepdims=True)
        xn = ((x * lax.rsqrt(ms + EPS)) * g_ref[...]).astype(BF16)

        def proj(j):
            return jnp.dot(xn, w_ref[:, j * D_MODEL:(j + 1) * D_MODEL], preferred_element_type=F32)

        c, lo, hi = jnp.tile(rc_ref[rows, :], reps), jnp.tile(rlo_ref[rows, :], reps), jnp.tile(rhi_ref[rows, :], reps)
        q_ref[rows, :] = (_rope(proj(0), c, lo, hi) * (HEAD_DIM ** -0.5 * math.log2(math.e))).astype(BF16)
        k_ref[rows, :] = _rope(proj(1), c, lo, hi).astype(BF16)
        v_ref[rows, :] = proj(2).astype(BF16)
        z_ref[rows, :] = (proj(3) * jax.nn.sigmoid(proj(4))).astype(BF16)
        ga_ref[rows, :] = jax.nn.sigmoid(proj(5)).astype(BF16)
        gc_ref[rows, :] = jax.nn.sigmoid(proj(6)).astype(BF16)


def _inproj(x2d, g, w_bf16, rc, rlo, rhi, tm):
    rows = x2d.shape[0]
    tab_blocks = rc.shape[0] // tm
    row_spec = pl.BlockSpec((tm, D_MODEL), lambda i: (i, 0))
    tab_spec = pl.BlockSpec((tm, LANES), lambda i: (i % tab_blocks, 0))
    out = jax.ShapeDtypeStruct((rows, D_MODEL), BF16)
    return pl.pallas_call(
        _inproj_body,
        grid=(rows // tm,),
        in_specs=[row_spec,
                  pl.BlockSpec((1, D_MODEL), lambda i: (0, 0)),
                  pl.BlockSpec((D_MODEL, N_PROJ * D_MODEL), lambda i: (0, 0), pipeline_mode=pl.Buffered(1)),
                  tab_spec, tab_spec, tab_spec],
        out_specs=[row_spec] * 6,
        out_shape=[out] * 6,
        compiler_params=_cparams(("arbitrary",)),
        name="inproj",
    )(x2d, g, w_bf16, rc, rlo, rhi)


def _rope_tables(length):
    half = ROT_DIM // 2
    inv_freq = np.float32(ROPE_THETA) ** (-np.arange(0, ROT_DIM, 2, dtype=np.float32) / np.float32(ROT_DIM))
    ang = (np.arange(length, dtype=np.float32)[:, None] * inv_freq[None, :]).astype(np.float32)
    cos, sin = np.cos(ang).astype(np.float32), np.sin(ang).astype(np.float32)
    ones = np.ones((length, HEAD_DIM - ROT_DIM), np.float32)
    c = np.concatenate([cos, cos, ones], axis=1)
    lo = np.concatenate([-sin, np.zeros((length, HEAD_DIM - half), np.float32)], axis=1)
    hi = np.concatenate([np.zeros((length, half), np.float32), sin,
                         np.zeros((length, HEAD_DIM - ROT_DIM), np.float32)], axis=1)
    rep = LANES // HEAD_DIM
    return tuple(jnp.asarray(np.tile(t, (1, rep))) for t in (c, lo, hi))


def _dot_nt(a, b):
    return lax.dot_general(a, b, (((1,), (1,)), ((), ())), preferred_element_type=F32)


def _attn_body(lq1_ref, lk1_ref, lq2_ref, lk2_ref, sg_ref, q_ref, k_ref, v_ref, km_ref, vm_ref,
               o_ref, *, seq, tq, tk_max, lam_init):
    lam = (jnp.exp(jnp.sum(lq1_ref[...] * lk1_ref[...], keepdims=True))
           - jnp.exp(jnp.sum(lq2_ref[...] * lk2_ref[...], keepdims=True)) + lam_init)
    lane = lax.broadcasted_iota(jnp.int32, (tq, LANES), 1)
    lane2 = lax.broadcasted_iota(jnp.int32, (2 * tq, LANES), 1)
    neg = -jnp.inf
    km = km_ref[...]
    vm = vm_ref[...]
    vm_ext = jnp.concatenate([vm, (lax.broadcasted_iota(jnp.int32, vm.shape, 0) < N_META).astype(BF16)], axis=1)

    for qi in range(seq // tq):
        q = q_ref[qi * tq:(qi + 1) * tq, :]
        zero = jnp.zeros_like(q)
        q2 = jnp.concatenate([jnp.where(lane < HEAD_DIM, q, zero), jnp.where(lane >= HEAD_DIM, q, zero)], axis=0)
        n_keys = (qi + 1) * tq
        blocks = [(start, min(tk_max, n_keys - start)) for start in range(0, n_keys, tk_max)]

        def scores(start, width):
            s = _dot_nt(q2, k_ref[start:start + width, :])
            if start + width > qi * tq:
                row_chunk = (qi * tq + lax.broadcasted_iota(jnp.int32, s.shape, 0) % tq) // CHUNK
                col_chunk = (start + lax.broadcasted_iota(jnp.int32, s.shape, 1)) // CHUNK
                s = jnp.where(col_chunk <= row_chunk, s, neg)
            tiles = [s[:, t * LANES:(t + 1) * LANES] for t in range(width // LANES)]
            if start == 0:
                tiles.append(jnp.where(lane2 < N_META, _dot_nt(q2, km), neg))
            return tiles

        m = acc = None
        tiles_next = scores(*blocks[0])
        for bi, (start, width) in enumerate(blocks):
            tiles = tiles_next
            if bi + 1 < len(blocks):
                tiles_next = scores(*blocks[bi + 1])
            tile_max = functools.reduce(jnp.maximum, tiles)
            m_blk = jnp.broadcast_to(jnp.max(tile_max, axis=-1, keepdims=True), tile_max.shape)
            m_new = m_blk if m is None else jnp.maximum(m, m_blk)
            p_tiles = [jnp.exp2(t - m_new) for t in tiles]
            n_real = width // LANES
            vb = v_ref[start:start + width, :]
            v_ext = jnp.concatenate([vb, jnp.ones_like(vb)], axis=1)
            pv = jnp.dot(jnp.concatenate(p_tiles[:n_real], axis=1).astype(BF16), v_ext,
                         preferred_element_type=F32)
            if start == 0:
                pv = pv + jnp.dot(p_tiles[n_real].astype(BF16), vm_ext, preferred_element_type=F32)
            if m is None:
                acc = pv
            else:
                alpha = jnp.exp2(m - m_new)
                acc = jnp.concatenate([alpha, alpha], axis=1) * acc + pv
            m = m_new

        o_maps = acc[:, :LANES] / acc[:, LANES:]
        o = o_maps[:tq] - lam * o_maps[tq:]
        ms = jnp.mean(o * o, axis=-1, keepdims=True)
        o = ((o * lax.rsqrt(ms + EPS)) * sg_ref[...]) * (1.0 - lam_init)
        o_ref[qi * tq:(qi + 1) * tq, :] = o.astype(BF16)


def _attention(q, k, v, k_meta, v_meta, lq1, lk1, lq2, lk2, subln_g, batch, seq, lam_init):
    tq = min(256, seq)
    head_spec = pl.BlockSpec((seq, LANES), lambda b, h: (b, h))
    meta_spec = pl.BlockSpec((LANES, LANES), lambda b, h: (0, h))
    lam_spec = pl.BlockSpec((1, HEAD_DIM), lambda b, h: (0, 0))
    pad = jnp.zeros((LANES - N_META, k_meta.shape[1]), k_meta.dtype)
    k_meta = jnp.concatenate([k_meta, pad], axis=0)
    v_meta = jnp.concatenate([v_meta, pad], axis=0)
    body = functools.partial(_attn_body, seq=seq, tq=tq, tk_max=512, lam_init=lam_init)
    return pl.pallas_call(
        body,
        grid=(batch, N_HEADS),
        in_specs=[lam_spec] * 4 + [pl.BlockSpec((1, LANES), lambda b, h: (0, 0)),
                                   head_spec, head_spec, head_spec, meta_spec, meta_spec],
        out_specs=head_spec,
        out_shape=jax.ShapeDtypeStruct(q.shape, BF16),
        compiler_params=_cparams(("arbitrary", "arbitrary")),
        name="attn",
    )(lq1, lk1, lq2, lk2, subln_g, q, k, v, k_meta, v_meta)


def _post_body(o_ref, z_ref, zprev_ref, zmeta_ref, ga_ref, gc_ref, x_ref,
               wo_ref, wpw_ref, wout_ref, cw_ref, cb_ref, lng_ref, lnb_ref, n2g_ref, wr_ref, br_ref,
               hres_ref, h2_ref, route_ref, zext_ref, zph_ref, conv_ref, zc_ref, *, tm, tiles_per_seq, conv_rows):
    i = pl.program_id(0)
    first = (i % tiles_per_seq) == 0
    meta_halo = jnp.concatenate(
        [jnp.zeros((CONV_HALO - N_META, D_MODEL), F32), zmeta_ref[...].astype(F32)], axis=0)
    zext_ref[0:CONV_HALO, :] = jnp.where(first, meta_halo, zprev_ref[...].astype(F32))
    zext_ref[CONV_HALO:, :] = z_ref[...].astype(F32)
    ph_rows = zph_ref.shape[1]
    for b in range(1, SUBLANES):
        zph_ref[b - 1] = zext_ref[b:b + ph_rows, :]

    base = CONV_HALO - (CONV_K - 1)
    reps = conv_rows // SUBLANES
    for j in range(D_MODEL // LANES):
        cols = slice(j * LANES, (j + 1) * LANES)
        for c in range(tm // conv_rows):
            r0 = c * conv_rows
            acc = jnp.zeros((conv_rows, LANES), F32)
            for kk in range(CONV_K):
                a, b = divmod(base + kk, SUBLANES)
                lo = r0 + a * SUBLANES
                win = zext_ref[lo:lo + conv_rows, cols] if b == 0 else zph_ref[b - 1, lo:lo + conv_rows, cols]
                acc = acc + win * jnp.tile(cw_ref[kk, :, cols], (reps, 1))
            conv_ref[r0:r0 + conv_rows, cols] = acc

    cb = cb_ref[...]
    lng = lng_ref[...]
    lnb = lnb_ref[...]
    ln_rows = 32
    for c in range(tm // ln_rows):
        r0 = c * ln_rows
        zc = conv_ref[r0:r0 + ln_rows, :] + cb
        mu = jnp.mean(zc, axis=-1, keepdims=True)
        var = jnp.mean(jnp.square(zc - mu), axis=-1, keepdims=True)
        y = ((zc - mu) * lax.rsqrt(var + EPS)) * lng + lnb
        zc_ref[r0:r0 + ln_rows, :] = (y * jax.nn.sigmoid(y)).astype(BF16)

    y_attn = jnp.dot(o_ref[...], wo_ref[...], preferred_element_type=F32)
    y_conv = jnp.dot(zc_ref[...], wpw_ref[...], preferred_element_type=F32)
    mix = ga_ref[...] * y_attn.astype(BF16) + gc_ref[...] * y_conv.astype(BF16)
    h_res = x_ref[...] + jnp.dot(mix, wout_ref[...], preferred_element_type=F32)
    hres_ref[...] = h_res

    ms = jnp.mean(h_res * h_res, axis=-1, keepdims=True)
    h2 = (h_res * lax.rsqrt(ms + EPS)) * n2g_ref[...]
    bits = lax.bitcast_convert_type(h2.astype(BF16).astype(F32), jnp.uint32)
    half = D_MODEL // 2
    packed = bits[:, :half] | (bits[:, half:] >> 16)
    for s in range(PACK_TILES):
        h2_ref[pl.ds(s, tm, stride=PACK_TILES), :] = packed[:, s * LANES:(s + 1) * LANES]

    logits = jnp.dot(h2.astype(BF16), wr_ref[...], preferred_element_type=F32) + br_ref[...]
    lane = lax.broadcasted_iota(jnp.int32, (tm, LANES), 1)
    neg = -jnp.inf
    big = jnp.int32(1 << 20)
    gl = jnp.where((lane >= N_EXPERTS) & (lane < N_EXPERTS + N_GROUPS), logits, neg)
    gmax = jnp.max(gl, axis=-1, keepdims=True)
    gidx = jnp.min(jnp.where(gl == gmax, lane - N_EXPERTS, big), axis=-1, keepdims=True)
    g_w = 1.0 / jnp.sum(jnp.exp(gl - gmax), axis=-1, keepdims=True)
    el = jnp.where((lane < N_EXPERTS) & ((lane // EXPERTS_PER_GROUP) == gidx), logits, neg)
    e1 = jnp.max(el, axis=-1, keepdims=True)
    i1 = jnp.min(jnp.where(el == e1, lane, big), axis=-1, keepdims=True)
    el2 = jnp.where(lane == i1, neg, el)
    e2 = jnp.max(el2, axis=-1, keepdims=True)
    i2 = jnp.min(jnp.where(el2 == e2, lane, big), axis=-1, keepdims=True)
    esum = jnp.sum(jnp.exp(el - e1), axis=-1, keepdims=True)
    p1 = 1.0 / esum
    p2 = jnp.exp(e2 - e1) / esum
    w1 = p1 / (p1 + p2)
    w2 = p2 / (p1 + p2)
    route = jnp.where(lane == 0, g_w * w1, 0.0)
    route = jnp.where(lane == 1, g_w * w2, route)
    route = jnp.where(lane == 2, i1.astype(F32), route)
    route = jnp.where(lane == 3, i2.astype(F32), route)
    route_ref[...] = route


def _post(o, z, z_meta, ga, gc, x2d, wo, wpw, wout, cw, cb, lng, lnb, n2g, wr, br, seq):
    rows = x2d.shape[0]
    tm = min(512, seq)
    tiles_per_seq = seq // tm
    conv_rows = min(64, tm)
    row_spec = pl.BlockSpec((tm, D_MODEL), lambda i: (i, 0))
    full = lambda shape: pl.BlockSpec(shape, lambda i: (0, 0))
    weight = pl.BlockSpec((D_MODEL, D_MODEL), lambda i: (0, 0), pipeline_mode=pl.Buffered(1))
    halo_blocks = tm // CONV_HALO
    cw = jnp.broadcast_to(cw[:, None, :], (CONV_K, SUBLANES, D_MODEL))
    body = functools.partial(_post_body, tm=tm, tiles_per_seq=tiles_per_seq, conv_rows=conv_rows)
    return pl.pallas_call(
        body,
        grid=(rows // tm,),
        in_specs=[row_spec, row_spec,
                  pl.BlockSpec((CONV_HALO, D_MODEL), lambda i: (jnp.maximum(i * halo_blocks - 1, 0), 0)),
                  full((N_META, D_MODEL)), row_spec, row_spec, row_spec,
                  weight, weight, weight,
                  pl.BlockSpec((CONV_K, SUBLANES, D_MODEL), lambda i: (0, 0, 0)),
                  full((1, D_MODEL)), full((1, D_MODEL)), full((1, D_MODEL)),
                  full((1, D_MODEL)), full((D_MODEL, LANES)), full((1, LANES))],
        out_specs=[row_spec, pl.BlockSpec((tm * PACK_TILES, LANES), lambda i: (i, 0)),
                   pl.BlockSpec((tm, LANES), lambda i: (i, 0))],
        out_shape=[jax.ShapeDtypeStruct((rows, D_MODEL), F32),
                   jax.ShapeDtypeStruct((rows * PACK_TILES, LANES), jnp.uint32),
                   jax.ShapeDtypeStruct((rows, LANES), F32)],
        scratch_shapes=[pltpu.VMEM((tm + CONV_HALO, D_MODEL), F32),
                        pltpu.VMEM((SUBLANES - 1, tm + CONV_HALO - SUBLANES, D_MODEL), F32),
                        pltpu.VMEM((tm, D_MODEL), F32),
                        pltpu.VMEM((tm, D_MODEL), BF16)],
        compiler_params=_cparams(("arbitrary",), VMEM_LIMIT_POST),
        name="post",
    )(o, z, z, z_meta, ga, gc, x2d, wo, wpw, wout, cw, cb, lng, lnb, n2g, wr, br)


ROW_UNROLL = 4


def _tile_rows(start, n_rows=1, tiles=ROW_TILES):
    return pl.ds(pl.multiple_of(start * tiles, tiles), n_rows * tiles)


def _ffn_body(blk_e_ref, nact_ref, tok_ref, tok_next_ref, dst_ref, h2_hbm, wg_ref, wu_ref, wd_ref, y_hbm,
              xg_ref, yb_ref, gsem, ssem):
    i = pl.program_id(0)
    nact = nact_ref[0]
    slot = i % 2

    def start_gather(t_ref, sl):
        def body(j, c):
            for pr in range(2):
                r = 2 * j + pr
                pltpu.make_async_copy(h2_hbm.at[_tile_rows(t_ref[0, 0, r], 1, PACK_TILES), :],
                                      xg_ref.at[sl, _tile_rows(r, 1, PACK_TILES), :], gsem.at[sl]).start(priority=pr)
            return c
        lax.fori_loop(0, MOE_BLOCK // 2, body, 0, unroll=ROW_UNROLL)

    def wait_gather(sl):
        pltpu.make_async_copy(h2_hbm.at[_tile_rows(0, MOE_BLOCK, PACK_TILES), :], xg_ref.at[sl], gsem.at[sl]).wait()

    def start_scatter():
        def body(j, c):
            for pr in range(2):
                r = 2 * j + pr
                pltpu.make_async_copy(yb_ref.at[_tile_rows(r), :], y_hbm.at[_tile_rows(dst_ref[0, 0, r]), :],
                                      ssem).start(priority=pr)
            return c
        lax.fori_loop(0, MOE_BLOCK // 2, body, 0, unroll=ROW_UNROLL)

    def wait_scatter():
        pltpu.make_async_copy(yb_ref, y_hbm.at[_tile_rows(0, MOE_BLOCK), :], ssem).wait()

    @pl.when(i == 0)
    def _():
        yb_ref[...] = jnp.zeros_like(yb_ref)
        n_rows = y_hbm.shape[0] // ROW_TILES
        spare = pltpu.make_async_copy(yb_ref, y_hbm.at[_tile_rows(n_rows - MOE_BLOCK, MOE_BLOCK), :], ssem)
        spare.start()
        spare.wait()

    @pl.when((i == 0) & (nact > 0))
    def _():
        start_gather(tok_ref, 0)

    @pl.when(i < nact)
    def _():
        wait_gather(slot)

        @pl.when(i + 1 < nact)
        def _():
            start_gather(tok_next_ref, 1 - slot)

        words = jnp.concatenate([xg_ref[slot, pl.ds(s, MOE_BLOCK, stride=PACK_TILES), :] for s in range(PACK_TILES)],
                                axis=1)
        x_hi = lax.bitcast_convert_type(words & jnp.uint32(0xFFFF0000), F32)
        x_lo = lax.bitcast_convert_type(words << 16, F32)
        xb = jnp.concatenate([x_hi, x_lo], axis=1).astype(BF16)
        g = jnp.dot(xb, wg_ref[0].astype(BF16), preferred_element_type=F32)
        u = jnp.dot(xb, wu_ref[0].astype(BF16), preferred_element_type=F32)
        hmid = ((g * jax.nn.sigmoid(g)) * u).astype(BF16)
        y = jnp.dot(hmid, wd_ref[0].astype(BF16), preferred_element_type=F32)

        @pl.when(i > 0)
        def _():
            wait_scatter()

        for s in range(ROW_TILES):
            yb_ref[pl.ds(s, MOE_BLOCK, stride=ROW_TILES), :] = y[:, s * LANES:(s + 1) * LANES]
        start_scatter()

        @pl.when(i == nact - 1)
        def _():
            wait_scatter()


def _ffn(blk_e, nact, tok, dst, h2_tiles, wg, wu, wd, n_out_rows):
    nb = blk_e.shape[0]
    smem_block = (1, 1, MOE_BLOCK)
    cur = pl.BlockSpec(smem_block, lambda i, be, na: (i, 0, 0), memory_space=pltpu.SMEM)
    nxt = pl.BlockSpec(smem_block, lambda i, be, na: (jnp.minimum(i + 1, nb - 1), 0, 0), memory_space=pltpu.SMEM)
    grid_spec = pltpu.PrefetchScalarGridSpec(
        num_scalar_prefetch=2,
        grid=(nb,),
        in_specs=[cur, nxt, cur,
                  pl.BlockSpec(memory_space=pl.ANY),
                  pl.BlockSpec((1, D_MODEL, D_EXPERT), lambda i, be, na: (be[i], 0, 0)),
                  pl.BlockSpec((1, D_MODEL, D_EXPERT), lambda i, be, na: (be[i], 0, 0)),
                  pl.BlockSpec((1, D_EXPERT, D_MODEL), lambda i, be, na: (be[i], 0, 0))],
        out_specs=pl.BlockSpec(memory_space=pl.ANY),
        scratch_shapes=[pltpu.VMEM((2, MOE_BLOCK * PACK_TILES, LANES), jnp.uint32),
                        pltpu.VMEM((MOE_BLOCK * ROW_TILES, LANES), F32),
                        pltpu.SemaphoreType.DMA((2,)), pltpu.SemaphoreType.DMA(())],
    )
    return pl.pallas_call(
        _ffn_body,
        grid_spec=grid_spec,
        out_shape=jax.ShapeDtypeStruct((n_out_rows * ROW_TILES, LANES), F32),
        compiler_params=_cparams(("arbitrary",)),
        name="ffn",
    )(blk_e, nact, tok, tok, dst, h2_tiles, wg, wu, wd)


def _dispatch(route, n_tok):
    eid = route[:, 2:2 + TOP_K].astype(jnp.int32)
    n_assign = n_tok * TOP_K
    flat_e = eid.reshape(n_assign)
    order = jnp.argsort(flat_e, stable=True).astype(jnp.int32)
    experts = jnp.arange(N_EXPERTS, dtype=jnp.int32)
    counts = jnp.sum((flat_e[None, :] == experts[:, None]).astype(jnp.int32), axis=1)
    starts = jnp.cumsum(counts) - counts
    pcounts = (counts + MOE_BLOCK - 1) // MOE_BLOCK * MOE_BLOCK
    pends = jnp.cumsum(pcounts)
    pstarts = pends - pcounts
    nb = -(-n_assign // MOE_BLOCK) + N_EXPERTS
    blk_start = jnp.arange(nb, dtype=jnp.int32) * MOE_BLOCK
    blk_e = jnp.minimum(jnp.sum((pends[None, :] <= blk_start[:, None]).astype(jnp.int32), axis=1), N_EXPERTS - 1)
    blk_cnt = jnp.clip(counts[blk_e] - (blk_start - pstarts[blk_e]), 0, MOE_BLOCK).astype(jnp.int32)
    within = jnp.arange(MOE_BLOCK, dtype=jnp.int32)[None, :]
    valid = within < blk_cnt[:, None]
    src = jnp.clip((starts[blk_e] + blk_start - pstarts[blk_e])[:, None] + within, 0, n_assign - 1)
    assign = order[src]
    tok = jnp.where(valid, assign // TOP_K, 0).astype(jnp.int32)
    dst = jnp.where(valid, (assign % TOP_K) * n_tok + assign // TOP_K, TOP_K * n_tok + within).astype(jnp.int32)
    n_active = (pends[-1:] // MOE_BLOCK).astype(jnp.int32)
    return blk_e, n_active, tok.reshape(nb, 1, MOE_BLOCK), dst.reshape(nb, 1, MOE_BLOCK)


def _final_body(hres_ref, y0_ref, y1_ref, route_ref, g_ref, out_ref, *, tm):
    y0 = jnp.concatenate([y0_ref[pl.ds(s, tm, stride=ROW_TILES), :] for s in range(ROW_TILES)], axis=1)
    y1 = jnp.concatenate([y1_ref[pl.ds(s, tm, stride=ROW_TILES), :] for s in range(ROW_TILES)], axis=1)
    route = route_ref[...]
    h = hres_ref[...] + (y0 * route[:, 0:1] + y1 * route[:, 1:2])
    ms = jnp.mean(h * h, axis=-1, keepdims=True)
    out_ref[...] = (h * lax.rsqrt(ms + EPS)) * g_ref[...]


def _final(h_res, y_tiles, route, final_g, seq):
    rows = h_res.shape[0]
    tm = min(512, seq)
    nt = rows // tm
    row_spec = pl.BlockSpec((tm, D_MODEL), lambda i: (i, 0))
    return pl.pallas_call(
        functools.partial(_final_body, tm=tm),
        grid=(nt,),
        in_specs=[row_spec,
                  pl.BlockSpec((tm * ROW_TILES, LANES), lambda i: (i, 0)),
                  pl.BlockSpec((tm * ROW_TILES, LANES), lambda i: (i + nt, 0)),
                  pl.BlockSpec((tm, LANES), lambda i: (i, 0)),
                  pl.BlockSpec((1, D_MODEL), lambda i: (0, 0))],
        out_specs=row_spec,
        out_shape=jax.ShapeDtypeStruct((rows, D_MODEL), F32),
        compiler_params=_cparams(("arbitrary",)),
        name="final",
    )(h_res, y_tiles, y_tiles, route, final_g)


def kernel(x, meta_tokens, norm1_g, w_in, lam_q1, lam_k1, lam_q2, lam_k2, subln_g, w_o_attn, conv_w, conv_b,
           conv_ln_g, conv_ln_b, w_pw2, w_out, norm2_g, w_group, b_group, w_router, b_router, w_e_gate,
           w_e_up, w_e_down, final_g):
    batch, seq, _ = x.shape
    n_tok = batch * seq
    layer = 0
    lam_init = 0.8 - 0.6 * math.exp(-0.3 * layer)
    row = lambda a: a.reshape(1, -1).astype(F32)

    x2d = x.reshape(n_tok, D_MODEL)
    w_in_b = w_in[layer].astype(BF16)
    rc, rlo, rhi = _rope_tables(N_META + seq)
    tm_in = min(512, seq)
    q, k, v, z, ga, gc = _inproj(x2d, row(norm1_g[layer]), w_in_b, rc[N_META:], rlo[N_META:], rhi[N_META:], tm_in)
    _, k_meta, v_meta, z_meta, _, _ = _inproj(meta_tokens.astype(F32), row(norm1_g[layer]), w_in_b,
                                              rc[:N_META], rlo[:N_META], rhi[:N_META], N_META)

    o = _attention(q, k, v, k_meta, v_meta, row(lam_q1[layer]), row(lam_k1[layer]), row(lam_q2[layer]),
                   row(lam_k2[layer]), row(subln_g[layer]), batch, seq, lam_init)

    cw = conv_w[layer].astype(F32)
    pad_cols = LANES - N_EXPERTS - N_GROUPS
    wr = jnp.concatenate([w_router[layer].astype(F32), w_group[layer].astype(F32),
                          jnp.zeros((D_MODEL, pad_cols), F32)], axis=1).astype(BF16)
    br = jnp.concatenate([b_router[layer].astype(F32), b_group[layer].astype(F32),
                          jnp.zeros((pad_cols,), F32)]).reshape(1, LANES)
    h_res, h2_tiles, route = _post(o, z, z_meta, ga, gc, x2d, w_o_attn[layer].astype(BF16),
                                   w_pw2[layer].astype(BF16), w_out[layer].astype(BF16), cw,
                                   row(conv_b[layer]), row(conv_ln_g[layer]), row(conv_ln_b[layer]),
                                   row(norm2_g[layer]), wr, br, seq)

    blk_e, n_active, tok, dst = _dispatch(route, n_tok)
    y_tiles = _ffn(blk_e, n_active, tok, dst, h2_tiles, w_e_gate[layer], w_e_up[layer], w_e_down[layer],
                   TOP_K * n_tok + MOE_BLOCK)

    out = _final(h_res, y_tiles, route, row(final_g), seq)
    return out.reshape(batch, seq, D_MODEL)
```

```python
import functools
import math

import jax
import jax.numpy as jnp
import numpy as np
from jax import lax
from jax.experimental import pallas as pl
from jax.experimental.pallas import tpu as pltpu

F32 = jnp.float32
BF16 = jnp.bfloat16

D_MODEL = 1024
CHUNK = 64
N_META = 16
N_HEADS = 8
HEAD_DIM = 64
ROT_DIM = HEAD_DIM // 4
ROPE_THETA = 500000.0
ATTN_W = N_HEADS * 2 * HEAD_DIM
CONV_K = 31
N_GROUPS = 4
EXPERTS_PER_GROUP = 8
N_EXPERTS = N_GROUPS * EXPERTS_PER_GROUP
TOP_K = 2
D_EXPERT = D_MODEL // 2
MOE_BLOCK = 512
EPS = 1e-6
LANES = 128
SUBLANES = 8
ROW_TILES = D_MODEL // LANES
PACK_TILES = ROW_TILES // 2
N_PROJ = 7
CONV_HALO = 32
VMEM_LIMIT = 48 * 1024 * 1024
VMEM_LIMIT_POST = 56 * 1024 * 1024


def _cparams(sem, vmem_limit=VMEM_LIMIT):
    return pltpu.CompilerParams(dimension_semantics=sem, vmem_limit_bytes=vmem_limit)


def _rope(t, c, s_lo, s_hi):
    half = ROT_DIM // 2
    n = t.shape[-1]
    return t * c + pltpu.roll(t, n - half, 1) * s_lo + pltpu.roll(t, half, 1) * s_hi


def _inproj_body(x_ref, g_ref, w_ref, rc_ref, rlo_ref, rhi_ref,
                 q_ref, k_ref, v_ref, z_ref, ga_ref, gc_ref):
    tm = x_ref.shape[0]
    n_half = 2 if tm % 32 == 0 else 1
    reps = (1, D_MODEL // LANES)
    for h in range(n_half):
        rows = slice(h * (tm // n_half), (h + 1) * (tm // n_half))
        x = x_ref[rows, :]
        ms = jnp.mean(x * x, axis=-1, keepdims=True)
        xn = ((x * lax.rsqrt(ms + EPS)) * g_ref[...]).astype(BF16)

        def proj(j):
            return jnp.dot(xn, w_ref[:, j * D_MODEL:(j + 1) * D_MODEL], preferred_element_type=F32)

        c, lo, hi = jnp.tile(rc_ref[rows, :], reps), jnp.tile(rlo_ref[rows, :], reps), jnp.tile(rhi_ref[rows, :], reps)
        q_ref[rows, :] = (_rope(proj(0), c, lo, hi) * (HEAD_DIM ** -0.5 * math.log2(math.e))).astype(BF16)
        k_ref[rows, :] = _rope(proj(1), c, lo, hi).astype(BF16)
        v_ref[rows, :] = proj(2).astype(BF16)
        z_ref[rows, :] = (proj(3) * jax.nn.sigmoid(proj(4))).astype(BF16)
        ga_ref[rows, :] = jax.nn.sigmoid(proj(5)).astype(BF16)
        gc_ref[rows, :] = jax.nn.sigmoid(proj(6)).astype(BF16)


def _inproj(x2d, g, w_bf16, rc, rlo, rhi, tm):
    rows = x2d.shape[0]
    tab_blocks = rc.shape[0] // tm
    row_spec = pl.BlockSpec((tm, D_MODEL), lambda i: (i, 0))
    tab_spec = pl.BlockSpec((tm, LANES), lambda i: (i % tab_blocks, 0))
    out = jax.ShapeDtypeStruct((rows, D_MODEL), BF16)
    return pl.pallas_call(
        _inproj_body,
        grid=(rows // tm,),
        in_specs=[row_spec,
                  pl.BlockSpec((1, D_MODEL), lambda i: (0, 0)),
                  pl.BlockSpec((D_MODEL, N_PROJ * D_MODEL), lambda i: (0, 0), pipeline_mode=pl.Buffered(1)),
                  tab_spec, tab_spec, tab_spec],
        out_specs=[row_spec] * 6,
        out_shape=[out] * 6,
        compiler_params=_cparams(("arbitrary",)),
        name="inproj",
    )(x2d, g, w_bf16, rc, rlo, rhi)


def _rope_tables(length):
    half = ROT_DIM // 2
    inv_freq = np.float32(ROPE_THETA) ** (-np.arange(0, ROT_DIM, 2, dtype=np.float32) / np.float32(ROT_DIM))
    ang = (np.arange(length, dtype=np.float32)[:, None] * inv_freq[None, :]).astype(np.float32)
    cos, sin = np.cos(ang).astype(np.float32), np.sin(ang).astype(np.float32)
    ones = np.ones((length, HEAD_DIM - ROT_DIM), np.float32)
    c = np.concatenate([cos, cos, ones], axis=1)
    lo = np.concatenate([-sin, np.zeros((length, HEAD_DIM - half), np.float32)], axis=1)
    hi = np.concatenate([np.zeros((length, half), np.float32), sin,
                         np.zeros((length, HEAD_DIM - ROT_DIM), np.float32)], axis=1)
    rep = LANES // HEAD_DIM
    return tuple(jnp.asarray(np.tile(t, (1, rep))) for t in (c, lo, hi))


def _dot_nt(a, b):
    return lax.dot_general(a, b, (((1,), (1,)), ((), ())), preferred_element_type=F32)


def _attn_body(lq1_ref, lk1_ref, lq2_ref, lk2_ref, sg_ref, q_ref, k_ref, v_ref, km_ref, vm_ref,
               o_ref, *, seq, tq, tk_max, lam_init):
    lam = (jnp.exp(jnp.sum(lq1_ref[...] * lk1_ref[...], keepdims=True))
           - jnp.exp(jnp.sum(lq2_ref[...] * lk2_ref[...], keepdims=True)) + lam_init)
    lane = lax.broadcasted_iota(jnp.int32, (tq, LANES), 1)
    lane2 = lax.broadcasted_iota(jnp.int32, (2 * tq, LANES), 1)
    neg = -jnp.inf
    km = km_ref[...]
    vm = vm_ref[...]
    vm_ext = jnp.concatenate([vm, (lax.broadcasted_iota(jnp.int32, vm.shape, 0) < N_META).astype(BF16)], axis=1)

    for qi in range(seq // tq):
        q = q_ref[qi * tq:(qi + 1) * tq, :]
        zero = jnp.zeros_like(q)
        q2 = jnp.concatenate([jnp.where(lane < HEAD_DIM, q, zero), jnp.where(lane >= HEAD_DIM, q, zero)], axis=0)
        n_keys = (qi + 1) * tq
        blocks = [(start, min(tk_max, n_keys - start)) for start in range(0, n_keys, tk_max)]

        def scores(start, width):
            s = _dot_nt(q2, k_ref[start:start + width, :])
            if start + width > qi * tq:
                row_chunk = (qi * tq + lax.broadcasted_iota(jnp.int32, s.shape, 0) % tq) // CHUNK
                col_chunk = (start + lax.broadcasted_iota(jnp.int32, s.shape, 1)) // CHUNK
                s = jnp.where(col_chunk <= row_chunk, s, neg)
            tiles = [s[:, t * LANES:(t + 1) * LANES] for t in range(width // LANES)]
            if start == 0:
                tiles.append(jnp.where(lane2 < N_META, _dot_nt(q2, km), neg))
            return tiles

        m = acc = None
        tiles_next = scores(*blocks[0])
        for bi, (start, width) in enumerate(blocks):
            tiles = tiles_next
            if bi + 1 < len(blocks):
                tiles_next = scores(*blocks[bi + 1])
            tile_max = functools.reduce(jnp.maximum, tiles)
            m_blk = jnp.broadcast_to(jnp.max(tile_max, axis=-1, keepdims=True), tile_max.shape)
            m_new = m_blk if m is None else jnp.maximum(m, m_blk)
            p_tiles = [jnp.exp2(t - m_new) for t in tiles]
            n_real = width // LANES
            vb = v_ref[start:start + width, :]
            v_ext = jnp.concatenate([vb, jnp.ones_like(vb)], axis=1)
            pv = jnp.dot(jnp.concatenate(p_tiles[:n_real], axis=1).astype(BF16), v_ext,
                         preferred_element_type=F32)
            if start == 0:
                pv = pv + jnp.dot(p_tiles[n_real].astype(BF16), vm_ext, preferred_element_type=F32)
            if m is None:
                acc = pv
            else:
                alpha = jnp.exp2(m - m_new)
                acc = jnp.concatenate([alpha, alpha], axis=1) * acc + pv
            m = m_new

        o_maps = acc[:, :LANES] / acc[:, LANES:]
        o = o_maps[:tq] - lam * o_maps[tq:]
        ms = jnp.mean(o * o, axis=-1, keepdims=True)
        o = ((o * lax.rsqrt(ms + EPS)) * sg_ref[...]) * (1.0 - lam_init)
        o_ref[qi * tq:(qi + 1) * tq, :] = o.astype(BF16)


def _attention(q, k, v, k_meta, v_meta, lq1, lk1, lq2, lk2, subln_g, batch, seq, lam_init):
    tq = min(256, seq)
    head_spec = pl.BlockSpec((seq, LANES), lambda b, h: (b, h))
    meta_spec = pl.BlockSpec((LANES, LANES), lambda b, h: (0, h))
    lam_spec = pl.BlockSpec((1, HEAD_DIM), lambda b, h: (0, 0))
    pad = jnp.zeros((LANES - N_META, k_meta.shape[1]), k_meta.dtype)
    k_meta = jnp.concatenate([k_meta, pad], axis=0)
    v_meta = jnp.concatenate([v_meta, pad], axis=0)
    body = functools.partial(_attn_body, seq=seq, tq=tq, tk_max=512, lam_init=lam_init)
    return pl.pallas_call(
        body,
        grid=(batch, N_HEADS),
        in_specs=[lam_spec] * 4 + [pl.BlockSpec((1, LANES), lambda b, h: (0, 0)),
                                   head_spec, head_spec, head_spec, meta_spec, meta_spec],
        out_specs=head_spec,
        out_shape=jax.ShapeDtypeStruct(q.shape, BF16),
        compiler_params=_cparams(("arbitrary", "arbitrary")),
        name="attn",
    )(lq1, lk1, lq2, lk2, subln_g, q, k, v, k_meta, v_meta)


def _post_body(o_ref, z_ref, zprev_ref, zmeta_ref, ga_ref, gc_ref, x_ref,
               wo_ref, wpw_ref, wout_ref, cw_ref, cb_ref, lng_ref, lnb_ref, n2g_ref, wr_ref, br_ref,
               hres_ref, h2_ref, route_ref, zext_ref, zph_ref, conv_ref, zc_ref, *, tm, tiles_per_seq, conv_rows):
    i = pl.program_id(0)
    first = (i % tiles_per_seq) == 0
    meta_halo = jnp.concatenate(
        [jnp.zeros((CONV_HALO - N_META, D_MODEL), F32), zmeta_ref[...].astype(F32)], axis=0)
    zext_ref[0:CONV_HALO, :] = jnp.where(first, meta_halo, zprev_ref[...].astype(F32))
    zext_ref[CONV_HALO:, :] = z_ref[...].astype(F32)
    ph_rows = zph_ref.shape[1]
    for b in range(1, SUBLANES):
        zph_ref[b - 1] = zext_ref[b:b + ph_rows, :]

    base = CONV_HALO - (CONV_K - 1)
    reps = conv_rows // SUBLANES
    for j in range(D_MODEL // LANES):
        cols = slice(j * LANES, (j + 1) * LANES)
        for c in range(tm // conv_rows):
            r0 = c * conv_rows
            acc = jnp.zeros((conv_rows, LANES), F32)
            for kk in range(CONV_K):
                a, b = divmod(base + kk, SUBLANES)
                lo = r0 + a * SUBLANES
                win = zext_ref[lo:lo + conv_rows, cols] if b == 0 else zph_ref[b - 1, lo:lo + conv_rows, cols]
                acc = acc + win * jnp.tile(cw_ref[kk, :, cols], (reps, 1))
            conv_ref[r0:r0 + conv_rows, cols] = acc

    cb = cb_ref[...]
    lng = lng_ref[...]
    lnb = lnb_ref[...]
    ln_rows = 32
    for c in range(tm // ln_rows):
        r0 = c * ln_rows
        zc = conv_ref[r0:r0 + ln_rows, :] + cb
        mu = jnp.mean(zc, axis=-1, keepdims=True)
        var = jnp.mean(jnp.square(zc - mu), axis=-1, keepdims=True)
        y = ((zc - mu) * lax.rsqrt(var + EPS)) * lng + lnb
        zc_ref[r0:r0 + ln_rows, :] = (y * jax.nn.sigmoid(y)).astype(BF16)

    y_attn = jnp.dot(o_ref[...], wo_ref[...], preferred_element_type=F32)
    y_conv = jnp.dot(zc_ref[...], wpw_ref[...], preferred_element_type=F32)
    mix = ga_ref[...] * y_attn.astype(BF16) + gc_ref[...] * y_conv.astype(BF16)
    h_res = x_ref[...] + jnp.dot(mix, wout_ref[...], preferred_element_type=F32)
    hres_ref[...] = h_res

    ms = jnp.mean(h_res * h_res, axis=-1, keepdims=True)
    h2 = (h_res * lax.rsqrt(ms + EPS)) * n2g_ref[...]
    bits = lax.bitcast_convert_type(h2.astype(BF16).astype(F32), jnp.uint32)
    half = D_MODEL // 2
    packed = bits[:, :half] | (bits[:, half:] >> 16)
    for s in range(PACK_TILES):
        h2_ref[pl.ds(s, tm, stride=PACK_TILES), :] = packed[:, s * LANES:(s + 1) * LANES]

    logits = jnp.dot(h2.astype(BF16), wr_ref[...], preferred_element_type=F32) + br_ref[...]
    lane = lax.broadcasted_iota(jnp.int32, (tm, LANES), 1)
    neg = -jnp.inf
    big = jnp.int32(1 << 20)
    gl = jnp.where((lane >= N_EXPERTS) & (lane < N_EXPERTS + N_GROUPS), logits, neg)
    gmax = jnp.max(gl, axis=-1, keepdims=True)
    gidx = jnp.min(jnp.where(gl == gmax, lane - N_EXPERTS, big), axis=-1, keepdims=True)
    g_w = 1.0 / jnp.sum(jnp.exp(gl - gmax), axis=-1, keepdims=True)
    el = jnp.where((lane < N_EXPERTS) & ((lane // EXPERTS_PER_GROUP) == gidx), logits, neg)
    e1 = jnp.max(el, axis=-1, keepdims=True)
    i1 = jnp.min(jnp.where(el == e1, lane, big), axis=-1, keepdims=True)
    el2 = jnp.where(lane == i1, neg, el)
    e2 = jnp.max(el2, axis=-1, keepdims=True)
    i2 = jnp.min(jnp.where(el2 == e2, lane, big), axis=-1, keepdims=True)
    esum = jnp.sum(jnp.exp(el - e1), axis=-1, keepdims=True)
    p1 = 1.0 / esum
    p2 = jnp.exp(e2 - e1) / esum
    w1 = p1 / (p1 + p2)
    w2 = p2 / (p1 + p2)
    route = jnp.where(lane == 0, g_w * w1, 0.0)
    route = jnp.where(lane == 1, g_w * w2, route)
    route = jnp.where(lane == 2, i1.astype(F32), route)
    route = jnp.where(lane == 3, i2.astype(F32), route)
    route_ref[...] = route


def _post(o, z, z_meta, ga, gc, x2d, wo, wpw, wout, cw, cb, lng, lnb, n2g, wr, br, seq):
    rows = x2d.shape[0]
    tm = min(512, seq)
    tiles_per_seq = seq // tm
    conv_rows = min(128, tm)
    row_spec = pl.BlockSpec((tm, D_MODEL), lambda i: (i, 0))
    full = lambda shape: pl.BlockSpec(shape, lambda i: (0, 0))
    weight = pl.BlockSpec((D_MODEL, D_MODEL), lambda i: (0, 0), pipeline_mode=pl.Buffered(1))
    halo_blocks = tm // CONV_HALO
    cw = jnp.broadcast_to(cw[:, None, :], (CONV_K, SUBLANES, D_MODEL))
    body = functools.partial(_post_body, tm=tm, tiles_per_seq=tiles_per_seq, conv_rows=conv_rows)
    return pl.pallas_call(
        body,
        grid=(rows // tm,),
        in_specs=[row_spec, row_spec,
                  pl.BlockSpec((CONV_HALO, D_MODEL), lambda i: (jnp.maximum(i * halo_blocks - 1, 0), 0)),
                  full((N_META, D_MODEL)), row_spec, row_spec, row_spec,
                  weight, weight, weight,
                  pl.BlockSpec((CONV_K, SUBLANES, D_MODEL), lambda i: (0, 0, 0)),
                  full((1, D_MODEL)), full((1, D_MODEL)), full((1, D_MODEL)),
                  full((1, D_MODEL)), full((D_MODEL, LANES)), full((1, LANES))],
        out_specs=[row_spec, pl.BlockSpec((tm * PACK_TILES, LANES), lambda i: (i, 0)),
                   pl.BlockSpec((tm, LANES), lambda i: (i, 0))],
        out_shape=[jax.ShapeDtypeStruct((rows, D_MODEL), F32),
                   jax.ShapeDtypeStruct((rows * PACK_TILES, LANES), jnp.uint32),
                   jax.ShapeDtypeStruct((rows, LANES), F32)],
        scratch_shapes=[pltpu.VMEM((tm + CONV_HALO, D_MODEL), F32),
                        pltpu.VMEM((SUBLANES - 1, tm + CONV_HALO - SUBLANES, D_MODEL), F32),
                        pltpu.VMEM((tm, D_MODEL), F32),
                        pltpu.VMEM((tm, D_MODEL), BF16)],
        compiler_params=_cparams(("arbitrary",), VMEM_LIMIT_POST),
        name="post",
    )(o, z, z, z_meta, ga, gc, x2d, wo, wpw, wout, cw, cb, lng, lnb, n2g, wr, br)


ROW_UNROLL = 4


def _tile_rows(start, n_rows=1, tiles=ROW_TILES):
    return pl.ds(pl.multiple_of(start * tiles, tiles), n_rows * tiles)


def _ffn_body(blk_e_ref, nact_ref, tok_ref, tok_next_ref, dst_ref, h2_hbm, wg_ref, wu_ref, wd_ref, y_hbm,
              xg_ref, yb_ref, gsem, ssem):
    i = pl.program_id(0)
    nact = nact_ref[0]
    slot = i % 2

    def start_gather(t_ref, sl):
        def body(j, c):
            for pr in range(2):
                r = 2 * j + pr
                pltpu.make_async_copy(h2_hbm.at[_tile_rows(t_ref[0, 0, r], 1, PACK_TILES), :],
                                      xg_ref.at[sl, _tile_rows(r, 1, PACK_TILES), :], gsem.at[sl]).start(priority=pr)
            return c
        lax.fori_loop(0, MOE_BLOCK // 2, body, 0, unroll=ROW_UNROLL)

    def wait_gather(sl):
        pltpu.make_async_copy(h2_hbm.at[_tile_rows(0, MOE_BLOCK, PACK_TILES), :], xg_ref.at[sl], gsem.at[sl]).wait()

    def start_scatter():
        def body(j, c):
            for pr in range(2):
                r = 2 * j + pr
                pltpu.make_async_copy(yb_ref.at[_tile_rows(r), :], y_hbm.at[_tile_rows(dst_ref[0, 0, r]), :],
                                      ssem).start(priority=pr)
            return c
        lax.fori_loop(0, MOE_BLOCK // 2, body, 0, unroll=ROW_UNROLL)

    def wait_scatter():
        pltpu.make_async_copy(yb_ref, y_hbm.at[_tile_rows(0, MOE_BLOCK), :], ssem).wait()

    @pl.when(i == 0)
    def _():
        yb_ref[...] = jnp.zeros_like(yb_ref)
        n_rows = y_hbm.shape[0] // ROW_TILES
        spare = pltpu.make_async_copy(yb_ref, y_hbm.at[_tile_rows(n_rows - MOE_BLOCK, MOE_BLOCK), :], ssem)
        spare.start()
        spare.wait()

    @pl.when((i == 0) & (nact > 0))
    def _():
        start_gather(tok_ref, 0)

    @pl.when(i < nact)
    def _():
        wait_gather(slot)

        @pl.when(i + 1 < nact)
        def _():
            start_gather(tok_next_ref, 1 - slot)

        words = jnp.concatenate([xg_ref[slot, pl.ds(s, MOE_BLOCK, stride=PACK_TILES), :] for s in range(PACK_TILES)],
                                axis=1)
        x_hi = lax.bitcast_convert_type(words & jnp.uint32(0xFFFF0000), F32)
        x_lo = lax.bitcast_convert_type(words << 16, F32)
        xb = jnp.concatenate([x_hi, x_lo], axis=1).astype(BF16)
        g = jnp.dot(xb, wg_ref[0].astype(BF16), preferred_element_type=F32)
        u = jnp.dot(xb, wu_ref[0].astype(BF16), preferred_element_type=F32)
        hmid = ((g * jax.nn.sigmoid(g)) * u).astype(BF16)
        y = jnp.dot(hmid, wd_ref[0].astype(BF16), preferred_element_type=F32)

        @pl.when(i > 0)
        def _():
            wait_scatter()

        for s in range(ROW_TILES):
            yb_ref[pl.ds(s, MOE_BLOCK, stride=ROW_TILES), :] = y[:, s * LANES:(s + 1) * LANES]
        start_scatter()

        @pl.when(i == nact - 1)
        def _():
            wait_scatter()


def _ffn(blk_e, nact, tok, dst, h2_tiles, wg, wu, wd, n_out_rows):
    nb = blk_e.shape[0]
    smem_block = (1, 1, MOE_BLOCK)
    cur = pl.BlockSpec(smem_block, lambda i, be, na: (i, 0, 0), memory_space=pltpu.SMEM)
    nxt = pl.BlockSpec(smem_block, lambda i, be, na: (jnp.minimum(i + 1, nb - 1), 0, 0), memory_space=pltpu.SMEM)
    grid_spec = pltpu.PrefetchScalarGridSpec(
        num_scalar_prefetch=2,
        grid=(nb,),
        in_specs=[cur, nxt, cur,
                  pl.BlockSpec(memory_space=pl.ANY),
                  pl.BlockSpec((1, D_MODEL, D_EXPERT), lambda i, be, na: (be[i], 0, 0)),
                  pl.BlockSpec((1, D_MODEL, D_EXPERT), lambda i, be, na: (be[i], 0, 0)),
                  pl.BlockSpec((1, D_EXPERT, D_MODEL), lambda i, be, na: (be[i], 0, 0))],
        out_specs=pl.BlockSpec(memory_space=pl.ANY),
        scratch_shapes=[pltpu.VMEM((2, MOE_BLOCK * PACK_TILES, LANES), jnp.uint32),
                        pltpu.VMEM((MOE_BLOCK * ROW_TILES, LANES), F32),
                        pltpu.SemaphoreType.DMA((2,)), pltpu.SemaphoreType.DMA(())],
    )
    return pl.pallas_call(
        _ffn_body,
        grid_spec=grid_spec,
        out_shape=jax.ShapeDtypeStruct((n_out_rows * ROW_TILES, LANES), F32),
        compiler_params=_cparams(("arbitrary",)),
        name="ffn",
    )(blk_e, nact, tok, tok, dst, h2_tiles, wg, wu, wd)


def _dispatch(route, n_tok):
    eid = route[:, 2:2 + TOP_K].astype(jnp.int32)
    n_assign = n_tok * TOP_K
    flat_e = eid.reshape(n_assign)
    idx_bits = (n_assign - 1).bit_length()
    keys = jnp.sort((flat_e << idx_bits) | jnp.arange(n_assign, dtype=jnp.int32))
    order = keys & ((1 << idx_bits) - 1)
    experts = jnp.arange(N_EXPERTS, dtype=jnp.int32)
    counts = jnp.sum((flat_e[None, :] == experts[:, None]).astype(jnp.int32), axis=1)
    starts = jnp.cumsum(counts) - counts
    pcounts = (counts + MOE_BLOCK - 1) // MOE_BLOCK * MOE_BLOCK
    pends = jnp.cumsum(pcounts)
    pstarts = pends - pcounts
    nb = -(-n_assign // MOE_BLOCK) + N_EXPERTS
    blk_start = jnp.arange(nb, dtype=jnp.int32) * MOE_BLOCK
    blk_e = jnp.minimum(jnp.sum((pends[None, :] <= blk_start[:, None]).astype(jnp.int32), axis=1), N_EXPERTS - 1)
    blk_cnt = jnp.clip(counts[blk_e] - (blk_start - pstarts[blk_e]), 0, MOE_BLOCK).astype(jnp.int32)
    within = jnp.arange(MOE_BLOCK, dtype=jnp.int32)[None, :]
    valid = within < blk_cnt[:, None]
    src = jnp.clip((starts[blk_e] + blk_start - pstarts[blk_e])[:, None] + within, 0, n_assign - 1)
    assign = order[src]
    tok = jnp.where(valid, assign // TOP_K, 0).astype(jnp.int32)
    dst = jnp.where(valid, (assign % TOP_K) * n_tok + assign // TOP_K, TOP_K * n_tok + within).astype(jnp.int32)
    n_active = (pends[-1:] // MOE_BLOCK).astype(jnp.int32)
    return blk_e, n_active, tok.reshape(nb, 1, MOE_BLOCK), dst.reshape(nb, 1, MOE_BLOCK)


def _final_body(hres_ref, y0_ref, y1_ref, route_ref, g_ref, out_ref, *, tm):
    y0 = jnp.concatenate([y0_ref[pl.ds(s, tm, stride=ROW_TILES), :] for s in range(ROW_TILES)], axis=1)
    y1 = jnp.concatenate([y1_ref[pl.ds(s, tm, stride=ROW_TILES), :] for s in range(ROW_TILES)], axis=1)
    route = route_ref[...]
    h = hres_ref[...] + (y0 * route[:, 0:1] + y1 * route[:, 1:2])
    ms = jnp.mean(h * h, axis=-1, keepdims=True)
    out_ref[...] = (h * lax.rsqrt(ms + EPS)) * g_ref[...]


def _final(h_res, y_tiles, route, final_g, seq):
    rows = h_res.shape[0]
    tm = min(512, seq)
    nt = rows // tm
    row_spec = pl.BlockSpec((tm, D_MODEL), lambda i: (i, 0))
    return pl.pallas_call(
        functools.partial(_final_body, tm=tm),
        grid=(nt,),
        in_specs=[row_spec,
                  pl.BlockSpec((tm * ROW_TILES, LANES), lambda i: (i, 0)),
                  pl.BlockSpec((tm * ROW_TILES, LANES), lambda i: (i + nt, 0)),
                  pl.BlockSpec((tm, LANES), lambda i: (i, 0)),
                  pl.BlockSpec((1, D_MODEL), lambda i: (0, 0))],
        out_specs=row_spec,
        out_shape=jax.ShapeDtypeStruct((rows, D_MODEL), F32),
        compiler_params=_cparams(("arbitrary",)),
        name="final",
    )(h_res, y_tiles, y_tiles, route, final_g)


def kernel(x, meta_tokens, norm1_g, w_in, lam_q1, lam_k1, lam_q2, lam_k2, subln_g, w_o_attn, conv_w, conv_b,
           conv_ln_g, conv_ln_b, w_pw2, w_out, norm2_g, w_group, b_group, w_router, b_router, w_e_gate,
           w_e_up, w_e_down, final_g):
    batch, seq, _ = x.shape
    n_tok = batch * seq
    layer = 0
    lam_init = 0.8 - 0.6 * math.exp(-0.3 * layer)
    row = lambda a: a.reshape(1, -1).astype(F32)

    x2d = x.reshape(n_tok, D_MODEL)
    w_in_b = w_in[layer].astype(BF16)
    rc, rlo, rhi = _rope_tables(N_META + seq)
    tm_in = min(512, seq)
    q, k, v, z, ga, gc = _inproj(x2d, row(norm1_g[layer]), w_in_b, rc[N_META:], rlo[N_META:], rhi[N_META:], tm_in)
    _, k_meta, v_meta, z_meta, _, _ = _inproj(meta_tokens.astype(F32), row(norm1_g[layer]), w_in_b,
                                              rc[:N_META], rlo[:N_META], rhi[:N_META], N_META)

    o = _attention(q, k, v, k_meta, v_meta, row(lam_q1[layer]), row(lam_k1[layer]), row(lam_q2[layer]),
                   row(lam_k2[layer]), row(subln_g[layer]), batch, seq, lam_init)

    cw = conv_w[layer].astype(F32)
    pad_cols = LANES - N_EXPERTS - N_GROUPS
    wr = jnp.concatenate([w_router[layer].astype(F32), w_group[layer].astype(F32),
                          jnp.zeros((D_MODEL, pad_cols), F32)], axis=1).astype(BF16)
    br = jnp.concatenate([b_router[layer].astype(F32), b_group[layer].astype(F32),
                          jnp.zeros((pad_cols,), F32)]).reshape(1, LANES)
    h_res, h2_tiles, route = _post(o, z, z_meta, ga, gc, x2d, w_o_attn[layer].astype(BF16),
                                   w_pw2[layer].astype(BF16), w_out[layer].astype(BF16), cw,
                                   row(conv_b[layer]), row(conv_ln_g[layer]), row(conv_ln_b[layer]),
                                   row(norm2_g[layer]), wr, br, seq)

    blk_e, n_active, tok, dst = _dispatch(route, n_tok)
    y_tiles = _ffn(blk_e, n_active, tok, dst, h2_tiles, w_e_gate[layer], w_e_up[layer], w_e_down[layer],
                   TOP_K * n_tok + MOE_BLOCK)

    out = _final(h_res, y_tiles, route, row(final_g), seq)
    return out.reshape(batch, seq, D_MODEL)
```

```python
import functools
import math

import jax
import jax.numpy as jnp
import numpy as np
from jax import lax
from jax.experimental import pallas as pl
from jax.experimental.pallas import tpu as pltpu

F32 = jnp.float32
BF16 = jnp.bfloat16

D_MODEL = 1024
CHUNK = 64
N_META = 16
N_HEADS = 8
HEAD_DIM = 64
ROT_DIM = HEAD_DIM // 4
ROPE_THETA = 500000.0
ATTN_W = N_HEADS * 2 * HEAD_DIM
CONV_K = 31
N_GROUPS = 4
EXPERTS_PER_GROUP = 8
N_EXPERTS = N_GROUPS * EXPERTS_PER_GROUP
TOP_K = 2
D_EXPERT = D_MODEL // 2
MOE_BLOCK = 512
EPS = 1e-6
LANES = 128
SUBLANES = 8
ROW_TILES = D_MODEL // LANES
PACK_TILES = ROW_TILES // 2
N_PROJ = 7
CONV_HALO = 32
VMEM_LIMIT = 48 * 1024 * 1024
VMEM_LIMIT_POST = 56 * 1024 * 1024


def _cparams(sem, vmem_limit=VMEM_LIMIT):
    return pltpu.CompilerParams(dimension_semantics=sem, vmem_limit_bytes=vmem_limit)


def _rope(t, c, s_lo, s_hi):
    half = ROT_DIM // 2
    n = t.shape[-1]
    return t * c + pltpu.roll(t, n - half, 1) * s_lo + pltpu.roll(t, half, 1) * s_hi


def _inproj_body(x_ref, g_ref, w_ref, rc_ref, rlo_ref, rhi_ref,
                 q_ref, k_ref, v_ref, z_ref, ga_ref, gc_ref):
    tm = x_ref.shape[0]
    n_half = 2 if tm % 32 == 0 else 1
    reps = (1, D_MODEL // LANES)
    for h in range(n_half):
        rows = slice(h * (tm // n_half), (h + 1) * (tm // n_half))
        x = x_ref[rows, :]
        ms = jnp.mean(x * x, axis=-1, keepdims=True)
        xn = ((x * lax.rsqrt(ms + EPS)) * g_ref[...]).astype(BF16)

        def proj(j):
            return jnp.dot(xn, w_ref[:, j * D_MODEL:(j + 1) * D_MODEL], preferred_element_type=F32)

        c, lo, hi = jnp.tile(rc_ref[rows, :], reps), jnp.tile(rlo_ref[rows, :], reps), jnp.tile(rhi_ref[rows, :], reps)
        q_ref[rows, :] = (_rope(proj(0), c, lo, hi) * (HEAD_DIM ** -0.5 * math.log2(math.e))).astype(BF16)
        k_ref[rows, :] = _rope(proj(1), c, lo, hi).astype(BF16)
        v_ref[rows, :] = proj(2).astype(BF16)
        z_ref[rows, :] = (proj(3) * jax.nn.sigmoid(proj(4))).astype(BF16)
        ga_ref[rows, :] = jax.nn.sigmoid(proj(5)).astype(BF16)
        gc_ref[rows, :] = jax.nn.sigmoid(proj(6)).astype(BF16)


def _inproj(x2d, g, w_bf16, rc, rlo, rhi, tm):
    rows = x2d.shape[0]
    tab_blocks = rc.shape[0] // tm
    row_spec = pl.BlockSpec((tm, D_MODEL), lambda i: (i, 0))
    tab_spec = pl.BlockSpec((tm, LANES), lambda i: (i % tab_blocks, 0))
    out = jax.ShapeDtypeStruct((rows, D_MODEL), BF16)
    return pl.pallas_call(
        _inproj_body,
        grid=(rows // tm,),
        in_specs=[row_spec,
                  pl.BlockSpec((1, D_MODEL), lambda i: (0, 0)),
                  pl.BlockSpec((D_MODEL, N_PROJ * D_MODEL), lambda i: (0, 0), pipeline_mode=pl.Buffered(1)),
                  tab_spec, tab_spec, tab_spec],
        out_specs=[row_spec] * 6,
        out_shape=[out] * 6,
        compiler_params=_cparams(("arbitrary",)),
        name="inproj",
    )(x2d, g, w_bf16, rc, rlo, rhi)


def _rope_tables(length):
    half = ROT_DIM // 2
    inv_freq = np.float32(ROPE_THETA) ** (-np.arange(0, ROT_DIM, 2, dtype=np.float32) / np.float32(ROT_DIM))
    ang = (np.arange(length, dtype=np.float32)[:, None] * inv_freq[None, :]).astype(np.float32)
    cos, sin = np.cos(ang).astype(np.float32), np.sin(ang).astype(np.float32)
    ones = np.ones((length, HEAD_DIM - ROT_DIM), np.float32)
    c = np.concatenate([cos, cos, ones], axis=1)
    lo = np.concatenate([-sin, np.zeros((length, HEAD_DIM - half), np.float32)], axis=1)
    hi = np.concatenate([np.zeros((length, half), np.float32), sin,
                         np.zeros((length, HEAD_DIM - ROT_DIM), np.float32)], axis=1)
    rep = LANES // HEAD_DIM
    return tuple(jnp.asarray(np.tile(t, (1, rep))) for t in (c, lo, hi))


def _dot_nt(a, b):
    return lax.dot_general(a, b, (((1,), (1,)), ((), ())), preferred_element_type=F32)


HEADS_PER_STEP = 2


def _attn_body(lq1_ref, lk1_ref, lq2_ref, lk2_ref, sg_ref, q_ref, k_ref, v_ref, km_ref, vm_ref, o_ref, **static):
    for hh in range(HEADS_PER_STEP):
        cols = pl.ds(hh * LANES, LANES)
        _attn_head(lq1_ref, lk1_ref, lq2_ref, lk2_ref, sg_ref, q_ref.at[:, cols], k_ref.at[:, cols],
                   v_ref.at[:, cols], km_ref.at[:, cols], vm_ref.at[:, cols], o_ref.at[:, cols], **static)


def _attn_head(lq1_ref, lk1_ref, lq2_ref, lk2_ref, sg_ref, q_ref, k_ref, v_ref, km_ref, vm_ref,
               o_ref, *, seq, tq, tk_max, lam_init):
    lam = (jnp.exp(jnp.sum(lq1_ref[...] * lk1_ref[...], keepdims=True))
           - jnp.exp(jnp.sum(lq2_ref[...] * lk2_ref[...], keepdims=True)) + lam_init)
    lane = lax.broadcasted_iota(jnp.int32, (tq, LANES), 1)
    lane2 = lax.broadcasted_iota(jnp.int32, (2 * tq, LANES), 1)
    neg = -jnp.inf
    km = km_ref[...]
    vm = vm_ref[...]
    vm_ext = jnp.concatenate([vm, (lax.broadcasted_iota(jnp.int32, vm.shape, 0) < N_META).astype(BF16)], axis=1)

    for qi in range(seq // tq):
        q = q_ref[qi * tq:(qi + 1) * tq, :]
        zero = jnp.zeros_like(q)
        q2 = jnp.concatenate([jnp.where(lane < HEAD_DIM, q, zero), jnp.where(lane >= HEAD_DIM, q, zero)], axis=0)
        n_keys = (qi + 1) * tq
        blocks = [(start, min(tk_max, n_keys - start)) for start in range(0, n_keys, tk_max)]

        def scores(start, width):
            s = _dot_nt(q2, k_ref[start:start + width, :])
            if start + width > qi * tq:
                row_chunk = (qi * tq + lax.broadcasted_iota(jnp.int32, s.shape, 0) % tq) // CHUNK
                col_chunk = (start + lax.broadcasted_iota(jnp.int32, s.shape, 1)) // CHUNK
                s = jnp.where(col_chunk <= row_chunk, s, neg)
            tiles = [s[:, t * LANES:(t + 1) * LANES] for t in range(width // LANES)]
            if start == 0:
                tiles.append(jnp.where(lane2 < N_META, _dot_nt(q2, km), neg))
            return tiles

        m = acc = None
        tiles_next = scores(*blocks[0])
        for bi, (start, width) in enumerate(blocks):
            tiles = tiles_next
            if bi + 1 < len(blocks):
                tiles_next = scores(*blocks[bi + 1])
            tile_max = functools.reduce(jnp.maximum, tiles)
            m_blk = jnp.broadcast_to(jnp.max(tile_max, axis=-1, keepdims=True), tile_max.shape)
            m_new = m_blk if m is None else jnp.maximum(m, m_blk)
            p_tiles = [jnp.exp2(t - m_new) for t in tiles]
            n_real = width // LANES
            vb = v_ref[start:start + width, :]
            v_ext = jnp.concatenate([vb, jnp.ones_like(vb)], axis=1)
            pv = jnp.dot(jnp.concatenate(p_tiles[:n_real], axis=1).astype(BF16), v_ext,
                         preferred_element_type=F32)
            if start == 0:
                pv = pv + jnp.dot(p_tiles[n_real].astype(BF16), vm_ext, preferred_element_type=F32)
            if m is None:
                acc = pv
            else:
                alpha = jnp.exp2(m - m_new)
                acc = jnp.concatenate([alpha, alpha], axis=1) * acc + pv
            m = m_new

        o_maps = acc[:, :LANES] / acc[:, LANES:]
        o = o_maps[:tq] - lam * o_maps[tq:]
        ms = jnp.mean(o * o, axis=-1, keepdims=True)
        o = ((o * lax.rsqrt(ms + EPS)) * sg_ref[...]) * (1.0 - lam_init)
        o_ref[qi * tq:(qi + 1) * tq, :] = o.astype(BF16)


def _attention(q, k, v, k_meta, v_meta, lq1, lk1, lq2, lk2, subln_g, batch, seq, lam_init):
    tq = min(256, seq)
    head_spec = pl.BlockSpec((seq, HEADS_PER_STEP * LANES), lambda b, h: (b, h))
    meta_spec = pl.BlockSpec((LANES, HEADS_PER_STEP * LANES), lambda b, h: (0, h))
    lam_spec = pl.BlockSpec((1, HEAD_DIM), lambda b, h: (0, 0))
    pad = jnp.zeros((LANES - N_META, k_meta.shape[1]), k_meta.dtype)
    k_meta = jnp.concatenate([k_meta, pad], axis=0)
    v_meta = jnp.concatenate([v_meta, pad], axis=0)
    body = functools.partial(_attn_body, seq=seq, tq=tq, tk_max=512, lam_init=lam_init)
    return pl.pallas_call(
        body,
        grid=(batch, N_HEADS // HEADS_PER_STEP),
        in_specs=[lam_spec] * 4 + [pl.BlockSpec((1, LANES), lambda b, h: (0, 0)),
                                   head_spec, head_spec, head_spec, meta_spec, meta_spec],
        out_specs=head_spec,
        out_shape=jax.ShapeDtypeStruct(q.shape, BF16),
        compiler_params=_cparams(("arbitrary", "arbitrary")),
        name="attn",
    )(lq1, lk1, lq2, lk2, subln_g, q, k, v, k_meta, v_meta)


def _post_body(o_ref, z_ref, zprev_ref, zmeta_ref, ga_ref, gc_ref, x_ref,
               wo_ref, wpw_ref, wout_ref, cw_ref, cb_ref, lng_ref, lnb_ref, n2g_ref, wr_ref, br_ref,
               hres_ref, h2_ref, route_ref, zext_ref, zph_ref, conv_ref, zc_ref, *, tm, tiles_per_seq, conv_rows):
    i = pl.program_id(0)
    first = (i % tiles_per_seq) == 0
    meta_halo = jnp.concatenate(
        [jnp.zeros((CONV_HALO - N_META, D_MODEL), F32), zmeta_ref[...].astype(F32)], axis=0)
    zext_ref[0:CONV_HALO, :] = jnp.where(first, meta_halo, zprev_ref[...].astype(F32))
    zext_ref[CONV_HALO:, :] = z_ref[...].astype(F32)
    ph_rows = zph_ref.shape[1]
    for b in range(1, SUBLANES):
        zph_ref[b - 1] = zext_ref[b:b + ph_rows, :]

    base = CONV_HALO - (CONV_K - 1)
    reps = conv_rows // SUBLANES
    for j in range(D_MODEL // LANES):
        cols = slice(j * LANES, (j + 1) * LANES)
        for c in range(tm // conv_rows):
            r0 = c * conv_rows
            acc = jnp.zeros((conv_rows, LANES), F32)
            for kk in range(CONV_K):
                a, b = divmod(base + kk, SUBLANES)
                lo = r0 + a * SUBLANES
                win = zext_ref[lo:lo + conv_rows, cols] if b == 0 else zph_ref[b - 1, lo:lo + conv_rows, cols]
                acc = acc + win * jnp.tile(cw_ref[kk, :, cols], (reps, 1))
            conv_ref[r0:r0 + conv_rows, cols] = acc

    cb = cb_ref[...]
    lng = lng_ref[...]
    lnb = lnb_ref[...]
    ln_rows = 32
    for c in range(tm // ln_rows):
        r0 = c * ln_rows
        zc = conv_ref[r0:r0 + ln_rows, :] + cb
        mu = jnp.mean(zc, axis=-1, keepdims=True)
        var = jnp.mean(jnp.square(zc - mu), axis=-1, keepdims=True)
        y = ((zc - mu) * lax.rsqrt(var + EPS)) * lng + lnb
        zc_ref[r0:r0 + ln_rows, :] = (y * jax.nn.sigmoid(y)).astype(BF16)

    y_attn = jnp.dot(o_ref[...], wo_ref[...], preferred_element_type=F32)
    y_conv = jnp.dot(zc_ref[...], wpw_ref[...], preferred_element_type=F32)
    mix = ga_ref[...] * y_attn.astype(BF16) + gc_ref[...] * y_conv.astype(BF16)
    h_res = x_ref[...] + jnp.dot(mix, wout_ref[...], preferred_element_type=F32)
    hres_ref[...] = h_res

    ms = jnp.mean(h_res * h_res, axis=-1, keepdims=True)
    h2 = (h_res * lax.rsqrt(ms + EPS)) * n2g_ref[...]
    bits = lax.bitcast_convert_type(h2.astype(BF16).astype(F32), jnp.uint32)
    half = D_MODEL // 2
    packed = bits[:, :half] | (bits[:, half:] >> 16)
    for s in range(PACK_TILES):
        h2_ref[pl.ds(s, tm, stride=PACK_TILES), :] = packed[:, s * LANES:(s + 1) * LANES]

    logits = jnp.dot(h2.astype(BF16), wr_ref[...], preferred_element_type=F32) + br_ref[...]
    lane = lax.broadcasted_iota(jnp.int32, (tm, LANES), 1)
    neg = -jnp.inf
    big = jnp.int32(1 << 20)
    gl = jnp.where((lane >= N_EXPERTS) & (lane < N_EXPERTS + N_GROUPS), logits, neg)
    gmax = jnp.max(gl, axis=-1, keepdims=True)
    gidx = jnp.min(jnp.where(gl == gmax, lane - N_EXPERTS, big), axis=-1, keepdims=True)
    g_w = 1.0 / jnp.sum(jnp.exp(gl - gmax), axis=-1, keepdims=True)
    el = jnp.where((lane < N_EXPERTS) & ((lane // EXPERTS_PER_GROUP) == gidx), logits, neg)
    e1 = jnp.max(el, axis=-1, keepdims=True)
    i1 = jnp.min(jnp.where(el == e1, lane, big), axis=-1, keepdims=True)
    el2 = jnp.where(lane == i1, neg, el)
    e2 = jnp.max(el2, axis=-1, keepdims=True)
    i2 = jnp.min(jnp.where(el2 == e2, lane, big), axis=-1, keepdims=True)
    esum = jnp.sum(jnp.exp(el - e1), axis=-1, keepdims=True)
    p1 = 1.0 / esum
    p2 = jnp.exp(e2 - e1) / esum
    w1 = p1 / (p1 + p2)
    w2 = p2 / (p1 + p2)
    route = jnp.where(lane == 0, g_w * w1, 0.0)
    route = jnp.where(lane == 1, g_w * w2, route)
    route = jnp.where(lane == 2, i1.astype(F32), route)
    route = jnp.where(lane == 3, i2.astype(F32), route)
    route_ref[...] = route


def _post(o, z, z_meta, ga, gc, x2d, wo, wpw, wout, cw, cb, lng, lnb, n2g, wr, br, seq):
    rows = x2d.shape[0]
    tm = min(512, seq)
    tiles_per_seq = seq // tm
    conv_rows = min(128, tm)
    row_spec = pl.BlockSpec((tm, D_MODEL), lambda i: (i, 0))
    full = lambda shape: pl.BlockSpec(shape, lambda i: (0, 0))
    weight = pl.BlockSpec((D_MODEL, D_MODEL), lambda i: (0, 0), pipeline_mode=pl.Buffered(1))
    halo_blocks = tm // CONV_HALO
    cw = jnp.broadcast_to(cw[:, None, :], (CONV_K, SUBLANES, D_MODEL))
    body = functools.partial(_post_body, tm=tm, tiles_per_seq=tiles_per_seq, conv_rows=conv_rows)
    return pl.pallas_call(
        body,
        grid=(rows // tm,),
        in_specs=[row_spec, row_spec,
                  pl.BlockSpec((CONV_HALO, D_MODEL), lambda i: (jnp.maximum(i * halo_blocks - 1, 0), 0)),
                  full((N_META, D_MODEL)), row_spec, row_spec, row_spec,
                  weight, weight, weight,
                  pl.BlockSpec((CONV_K, SUBLANES, D_MODEL), lambda i: (0, 0, 0)),
                  full((1, D_MODEL)), full((1, D_MODEL)), full((1, D_MODEL)),
                  full((1, D_MODEL)), full((D_MODEL, LANES)), full((1, LANES))],
        out_specs=[row_spec, pl.BlockSpec((tm * PACK_TILES, LANES), lambda i: (i, 0)),
                   pl.BlockSpec((tm, LANES), lambda i: (i, 0))],
        out_shape=[jax.ShapeDtypeStruct((rows, D_MODEL), F32),
                   jax.ShapeDtypeStruct((rows * PACK_TILES, LANES), jnp.uint32),
                   jax.ShapeDtypeStruct((rows, LANES), F32)],
        scratch_shapes=[pltpu.VMEM((tm + CONV_HALO, D_MODEL), F32),
                        pltpu.VMEM((SUBLANES - 1, tm + CONV_HALO - SUBLANES, D_MODEL), F32),
                        pltpu.VMEM((tm, D_MODEL), F32),
                        pltpu.VMEM((tm, D_MODEL), BF16)],
        compiler_params=_cparams(("arbitrary",), VMEM_LIMIT_POST),
        name="post",
    )(o, z, z, z_meta, ga, gc, x2d, wo, wpw, wout, cw, cb, lng, lnb, n2g, wr, br)


ROW_UNROLL = 4


def _tile_rows(start, n_rows=1, tiles=ROW_TILES):
    return pl.ds(pl.multiple_of(start * tiles, tiles), n_rows * tiles)


def _ffn_body(blk_e_ref, nact_ref, tok_ref, tok_next_ref, dst_ref, h2_hbm, wg_ref, wu_ref, wd_ref, y_hbm,
              xg_ref, yb_ref, gsem, ssem):
    i = pl.program_id(0)
    nact = nact_ref[0]
    slot = i % 2

    def start_gather(t_ref, sl):
        def body(j, c):
            for pr in range(2):
                r = 2 * j + pr
                pltpu.make_async_copy(h2_hbm.at[_tile_rows(t_ref[0, 0, r], 1, PACK_TILES), :],
                                      xg_ref.at[sl, _tile_rows(r, 1, PACK_TILES), :], gsem.at[sl]).start(priority=pr)
            return c
        lax.fori_loop(0, MOE_BLOCK // 2, body, 0, unroll=ROW_UNROLL)

    def wait_gather(sl):
        pltpu.make_async_copy(h2_hbm.at[_tile_rows(0, MOE_BLOCK, PACK_TILES), :], xg_ref.at[sl], gsem.at[sl]).wait()

    def start_scatter():
        def body(j, c):
            for pr in range(2):
                r = 2 * j + pr
                pltpu.make_async_copy(yb_ref.at[_tile_rows(r), :], y_hbm.at[_tile_rows(dst_ref[0, 0, r]), :],
                                      ssem).start(priority=pr)
            return c
        lax.fori_loop(0, MOE_BLOCK // 2, body, 0, unroll=ROW_UNROLL)

    def wait_scatter():
        pltpu.make_async_copy(yb_ref, y_hbm.at[_tile_rows(0, MOE_BLOCK), :], ssem).wait()

    @pl.when(i == 0)
    def _():
        yb_ref[...] = jnp.zeros_like(yb_ref)
        n_rows = y_hbm.shape[0] // ROW_TILES
        spare = pltpu.make_async_copy(yb_ref, y_hbm.at[_tile_rows(n_rows - MOE_BLOCK, MOE_BLOCK), :], ssem)
        spare.start()
        spare.wait()

    @pl.when((i == 0) & (nact > 0))
    def _():
        start_gather(tok_ref, 0)

    @pl.when(i < nact)
    def _():
        wait_gather(slot)

        @pl.when(i + 1 < nact)
        def _():
            start_gather(tok_next_ref, 1 - slot)

        words = jnp.concatenate([xg_ref[slot, pl.ds(s, MOE_BLOCK, stride=PACK_TILES), :] for s in range(PACK_TILES)],
                                axis=1)
        x_hi = lax.bitcast_convert_type(words & jnp.uint32(0xFFFF0000), F32)
        x_lo = lax.bitcast_convert_type(words << 16, F32)
        xb = jnp.concatenate([x_hi, x_lo], axis=1).astype(BF16)
        g = jnp.dot(xb, wg_ref[0].astype(BF16), preferred_element_type=F32)
        u = jnp.dot(xb, wu_ref[0].astype(BF16), preferred_element_type=F32)
        hmid = ((g * jax.nn.sigmoid(g)) * u).astype(BF16)
        y = jnp.dot(hmid, wd_ref[0].astype(BF16), preferred_element_type=F32)

        @pl.when(i > 0)
        def _():
            wait_scatter()

        for s in range(ROW_TILES):
            yb_ref[pl.ds(s, MOE_BLOCK, stride=ROW_TILES), :] = y[:, s * LANES:(s + 1) * LANES]
        start_scatter()

        @pl.when(i == nact - 1)
        def _():
            wait_scatter()


def _ffn(blk_e, nact, tok, dst, h2_tiles, wg, wu, wd, n_out_rows):
    nb = blk_e.shape[0]
    smem_block = (1, 1, MOE_BLOCK)
    cur = pl.BlockSpec(smem_block, lambda i, be, na: (i, 0, 0), memory_space=pltpu.SMEM)
    nxt = pl.BlockSpec(smem_block, lambda i, be, na: (jnp.minimum(i + 1, nb - 1), 0, 0), memory_space=pltpu.SMEM)
    grid_spec = pltpu.PrefetchScalarGridSpec(
        num_scalar_prefetch=2,
        grid=(nb,),
        in_specs=[cur, nxt, cur,
                  pl.BlockSpec(memory_space=pl.ANY),
                  pl.BlockSpec((1, D_MODEL, D_EXPERT), lambda i, be, na: (be[i], 0, 0)),
                  pl.BlockSpec((1, D_MODEL, D_EXPERT), lambda i, be, na: (be[i], 0, 0)),
                  pl.BlockSpec((1, D_EXPERT, D_MODEL), lambda i, be, na: (be[i], 0, 0))],
        out_specs=pl.BlockSpec(memory_space=pl.ANY),
        scratch_shapes=[pltpu.VMEM((2, MOE_BLOCK * PACK_TILES, LANES), jnp.uint32),
                        pltpu.VMEM((MOE_BLOCK * ROW_TILES, LANES), F32),
                        pltpu.SemaphoreType.DMA((2,)), pltpu.SemaphoreType.DMA(())],
    )
    return pl.pallas_call(
        _ffn_body,
        grid_spec=grid_spec,
        out_shape=jax.ShapeDtypeStruct((n_out_rows * ROW_TILES, LANES), F32),
        compiler_params=_cparams(("arbitrary",)),
        name="ffn",
    )(blk_e, nact, tok, tok, dst, h2_tiles, wg, wu, wd)


def _dispatch(route, n_tok):
    eid = route[:, 2:2 + TOP_K].astype(jnp.int32)
    n_assign = n_tok * TOP_K
    flat_e = eid.reshape(n_assign)
    idx_bits = (n_assign - 1).bit_length()
    keys = jnp.sort((flat_e << idx_bits) | jnp.arange(n_assign, dtype=jnp.int32))
    order = keys & ((1 << idx_bits) - 1)
    experts = jnp.arange(N_EXPERTS, dtype=jnp.int32)
    counts = jnp.sum((flat_e[None, :] == experts[:, None]).astype(jnp.int32), axis=1)
    starts = jnp.cumsum(counts) - counts
    pcounts = (counts + MOE_BLOCK - 1) // MOE_BLOCK * MOE_BLOCK
    pends = jnp.cumsum(pcounts)
    pstarts = pends - pcounts
    nb = -(-n_assign // MOE_BLOCK) + N_EXPERTS
    blk_start = jnp.arange(nb, dtype=jnp.int32) * MOE_BLOCK
    blk_e = jnp.minimum(jnp.sum((pends[None, :] <= blk_start[:, None]).astype(jnp.int32), axis=1), N_EXPERTS - 1)
    blk_cnt = jnp.clip(counts[blk_e] - (blk_start - pstarts[blk_e]), 0, MOE_BLOCK).astype(jnp.int32)
    within = jnp.arange(MOE_BLOCK, dtype=jnp.int32)[None, :]
    valid = within < blk_cnt[:, None]
    src = jnp.clip((starts[blk_e] + blk_start - pstarts[blk_e])[:, None] + within, 0, n_assign - 1)
    assign = order[src]
    tok = jnp.where(valid, assign // TOP_K, 0).astype(jnp.int32)
    dst = jnp.where(valid, (assign % TOP_K) * n_tok + assign // TOP_K, TOP_K * n_tok + within).astype(jnp.int32)
    n_active = (pends[-1:] // MOE_BLOCK).astype(jnp.int32)
    return blk_e, n_active, tok.reshape(nb, 1, MOE_BLOCK), dst.reshape(nb, 1, MOE_BLOCK)


def _final_body(hres_ref, y0_ref, y1_ref, route_ref, g_ref, out_ref, *, tm):
    y0 = jnp.concatenate([y0_ref[pl.ds(s, tm, stride=ROW_TILES), :] for s in range(ROW_TILES)], axis=1)
    y1 = jnp.concatenate([y1_ref[pl.ds(s, tm, stride=ROW_TILES), :] for s in range(ROW_TILES)], axis=1)
    route = route_ref[...]
    h = hres_ref[...] + (y0 * route[:, 0:1] + y1 * route[:, 1:2])
    ms = jnp.mean(h * h, axis=-1, keepdims=True)
    out_ref[...] = (h * lax.rsqrt(ms + EPS)) * g_ref[...]


def _final(h_res, y_tiles, route, final_g, seq):
    rows = h_res.shape[0]
    tm = min(512, seq)
    nt = rows // tm
    row_spec = pl.BlockSpec((tm, D_MODEL), lambda i: (i, 0))
    return pl.pallas_call(
        functools.partial(_final_body, tm=tm),
        grid=(nt,),
        in_specs=[row_spec,
                  pl.BlockSpec((tm * ROW_TILES, LANES), lambda i: (i, 0)),
                  pl.BlockSpec((tm * ROW_TILES, LANES), lambda i: (i + nt, 0)),
                  pl.BlockSpec((tm, LANES), lambda i: (i, 0)),
                  pl.BlockSpec((1, D_MODEL), lambda i: (0, 0))],
        out_specs=row_spec,
        out_shape=jax.ShapeDtypeStruct((rows, D_MODEL), F32),
        compiler_params=_cparams(("arbitrary",)),
        name="final",
    )(h_res, y_tiles, y_tiles, route, final_g)


def kernel(x, meta_tokens, norm1_g, w_in, lam_q1, lam_k1, lam_q2, lam_k2, subln_g, w_o_attn, conv_w, conv_b,
           conv_ln_g, conv_ln_b, w_pw2, w_out, norm2_g, w_group, b_group, w_router, b_router, w_e_gate,
           w_e_up, w_e_down, final_g):
    batch, seq, _ = x.shape
    n_tok = batch * seq
    layer = 0
    lam_init = 0.8 - 0.6 * math.exp(-0.3 * layer)
    row = lambda a: a.reshape(1, -1).astype(F32)

    x2d = x.reshape(n_tok, D_MODEL)
    w_in_b = w_in[layer].astype(BF16)
    rc, rlo, rhi = _rope_tables(N_META + seq)
    tm_in = min(512, seq)
    q, k, v, z, ga, gc = _inproj(x2d, row(norm1_g[layer]), w_in_b, rc[N_META:], rlo[N_META:], rhi[N_META:], tm_in)
    _, k_meta, v_meta, z_meta, _, _ = _inproj(meta_tokens.astype(F32), row(norm1_g[layer]), w_in_b,
                                              rc[:N_META], rlo[:N_META], rhi[:N_META], N_META)

    o = _attention(q, k, v, k_meta, v_meta, row(lam_q1[layer]), row(lam_k1[layer]), row(lam_q2[layer]),
                   row(lam_k2[layer]), row(subln_g[layer]), batch, seq, lam_init)

    cw = conv_w[layer].astype(F32)
    pad_cols = LANES - N_EXPERTS - N_GROUPS
    wr = jnp.concatenate([w_router[layer].astype(F32), w_group[layer].astype(F32),
                          jnp.zeros((D_MODEL, pad_cols), F32)], axis=1).astype(BF16)
    br = jnp.concatenate([b_router[layer].astype(F32), b_group[layer].astype(F32),
                          jnp.zeros((pad_cols,), F32)]).reshape(1, LANES)
    h_res, h2_tiles, route = _post(o, z, z_meta, ga, gc, x2d, w_o_attn[layer].astype(BF16),
                                   w_pw2[layer].astype(BF16), w_out[layer].astype(BF16), cw,
                                   row(conv_b[layer]), row(conv_ln_g[layer]), row(conv_ln_b[layer]),
                                   row(norm2_g[layer]), wr, br, seq)

    blk_e, n_active, tok, dst = _dispatch(route, n_tok)
    y_tiles = _ffn(blk_e, n_active, tok, dst, h2_tiles, w_e_gate[layer], w_e_up[layer], w_e_down[layer],
                   TOP_K * n_tok + MOE_BLOCK)

    out = _final(h_res, y_tiles, route, row(final_g), seq)
    return out.reshape(batch, seq, D_MODEL)
```
